```python
import math
import jax, jax.numpy as jnp
from jax import lax
import numpy as np

D_MODEL = 2048
BATCH = 1
SEQ = 16384
DEPTH = 2

GRID_W = 64
CTX_LEN = 256
ROPE_BASE = 10000.0
EPS = 1e-6
Q_BLOCK = 128

H_A = 8
DA = 64
DV_A = 2 * DA
W_A = H_A * DV_A

H_M = 16
P_M = 64
D_INNER = H_M * P_M
N_STATE = 128
G_M = 2
D_CONV = 5
CHUNK = 128
CONV_CH = D_INNER + 2 * G_M * N_STATE

H_C = 8
D_NOPE = 128
D_ROPE = 64
D_VC = 128
Q_LORA = 512
KV_LORA = 256
W_C = H_C * D_VC
MLA_SCALE = (D_NOPE + D_ROPE) ** -0.5

D_FF = 4 * D_MODEL
N_BRANCH = 3
N_MOD = 6

IN_SIZES = (2 * H_A * DA, 2 * H_A * DA, W_A, D_INNER, CONV_CH, 2 * H_M,
            Q_LORA, KV_LORA, D_ROPE, N_BRANCH * D_MODEL)
P_IN = sum(IN_SIZES)

kernel_name = "hybrid_diff_ssd_mla_prefix_trunk"


def rmsnorm(x, g):
    xf = x.astype(jnp.float32)
    y = xf * lax.rsqrt(jnp.mean(xf * xf, axis=-1, keepdims=True) + EPS)
    return y.astype(x.dtype) * g


def modulate(s, g, shift, scale):
    return rmsnorm(s, g) * (1 + scale) + shift


def modulation(cond, w_ada, b_ada):
    m = jax.nn.silu(cond) @ w_ada + b_ada
    return jnp.split(m, N_MOD, axis=-1)


def split_in(p):
    offs, acc = [], 0
    for sz in IN_SIZES[:-1]:
        acc += sz
        offs.append(acc)
    return jnp.split(p, offs, axis=-1)


def axial_rope_tables(rows, dim, dtype):
    row = jnp.repeat(jnp.arange(rows), GRID_W).astype(jnp.float32)
    col = jnp.tile(jnp.arange(GRID_W), rows).astype(jnp.float32)
    n_freq = dim // 4
    inv = 1.0 / (ROPE_BASE ** (jnp.arange(n_freq, dtype=jnp.float32) / n_freq))
    ang = jnp.concatenate([row[:, None] * inv, col[:, None] * inv], axis=-1)
    return jnp.cos(ang).astype(dtype), jnp.sin(ang).astype(dtype)


def apply_rope(x, rope):
    cos, sin = rope
    half = x.shape[-1] // 2
    x1, x2 = x[..., :half], x[..., half:]
    c = cos[None, :, None, :]
    s = sin[None, :, None, :]
    return jnp.concatenate([x1 * c - x2 * s, x2 * c + x1 * s], axis=-1)


def softmax_attn(q, k, v, scale):
    s = jnp.einsum('bqhd,bthd->bhqt', q, k).astype(jnp.float32) * scale
    p = jax.nn.softmax(s, axis=-1).astype(v.dtype)
    return jnp.einsum('bhqt,bthe->bqhe', p, v)


def diff_attn_core(q1, q2, k1, k2, v, lam):
    scale = DA ** -0.5
    return softmax_attn(q1, k1, v, scale) - lam.astype(v.dtype) * softmax_attn(q2, k2, v, scale)


def over_query_blocks(fn, qs):
    b, s = qs[0].shape[:2]
    nb = s // Q_BLOCK
    blk = tuple(jnp.moveaxis(q.reshape((b, nb, Q_BLOCK) + q.shape[2:]), 1, 0) for q in qs)
    out = lax.map(lambda qb: fn(*qb), blk)
    return jnp.moveaxis(out, 0, 1).reshape((b, s) + out.shape[3:])


def dwconv(u, w, bias):
    ch = u.shape[-1]
    y = lax.conv_general_dilated(u, w.astype(u.dtype)[:, None, :], window_strides=(1,),
                                 padding=[(D_CONV // 2, D_CONV // 2)],
                                 dimension_numbers=('NWC', 'WIO', 'NWC'),
                                 feature_group_count=ch)
    return y + bias


def ssd_scan(x, dt, A, Bm, Cm, h0, need_y):
    b, s, h, p = x.shape
    nc = s // CHUNK
    xd = (x.astype(jnp.float32) * dt[..., None]).reshape(b, nc, CHUNK, h, p)
    Bc = Bm.astype(jnp.float32).reshape(b, nc, CHUNK, h, N_STATE)
    Cc = Cm.astype(jnp.float32).reshape(b, nc, CHUNK, h, N_STATE)
    a = jnp.moveaxis((dt * A).reshape(b, nc, CHUNK, h), 3, 1)
    a_cum = jnp.cumsum(a, axis=-1)
    decay_to_end = jnp.exp(a_cum[..., -1:] - a_cum)
    states = jnp.einsum('bclhn,bhcl,bclhp->bchpn', Bc, decay_to_end, xd)
    chunk_decay = jnp.exp(a_cum[..., -1])

    def step(hc, inp):
        dec, st = inp
        return dec[..., None, None] * hc + st, hc

    h_final, prev = lax.scan(step, h0.astype(jnp.float32),
                             (jnp.moveaxis(chunk_decay, 2, 0), jnp.moveaxis(states, 1, 0)))
    if not need_y:
        return None, h_final
    prev = jnp.moveaxis(prev, 0, 1)
    tril = jnp.tril(jnp.ones((CHUNK, CHUNK), dtype=bool))
    seg = jnp.where(tril, a_cum[..., :, None] - a_cum[..., None, :], -jnp.inf)
    cb = jnp.einsum('bclhn,bcshn->bhcls', Cc, Bc) * jnp.exp(seg)
    y_diag = jnp.einsum('bhcls,bcshp->bclhp', cb, xd)
    y_off = jnp.einsum('bclhn,bchpn,bhcl->bclhp', Cc, prev, jnp.exp(a_cum))
    return (y_diag + y_off).reshape(b, s, h, p).astype(x.dtype), h_final


def bidir_ssd(xs, Bm, Cm, dt_f, dt_b, A_f, A_b, h0_f, h0_b, need_y):
    y_f, hf = ssd_scan(xs, dt_f, A_f, Bm, Cm, h0_f, need_y)
    flip = lambda t: jnp.flip(t, axis=1)
    y_b, hb = ssd_scan(flip(xs), flip(dt_b), A_b, flip(Bm), flip(Cm), h0_b, need_y)
    y = y_f + flip(y_b) if need_y else None
    return y, hf, hb


def ssm_inputs(xbc_raw, dt_raw, conv_w, conv_b, dt_bias):
    b, s = xbc_raw.shape[:2]
    xbc = jax.nn.silu(dwconv(xbc_raw, conv_w, conv_b))
    xs = xbc[..., :D_INNER].reshape(b, s, H_M, P_M)
    grp = lambda t: jnp.repeat(t.reshape(b, s, G_M, N_STATE), H_M // G_M, axis=2)
    Bm = grp(xbc[..., D_INNER:D_INNER + G_M * N_STATE])
    Cm = grp(xbc[..., D_INNER + G_M * N_STATE:])
    dt32 = dt_raw.astype(jnp.float32)
    dtb32 = dt_bias.astype(jnp.float32)
    dt_f = jax.nn.softplus(dt32[..., :H_M] + dtb32[0])
    dt_b = jax.nn.softplus(dt32[..., H_M:] + dtb32[1])
    return xs, Bm, Cm, dt_f, dt_b


def ssm_out(y, xs, z, d_skip, norm_g, w_o):
    b, s = y.shape[:2]
    y = (y + xs * d_skip[:, None]).reshape(b, s, D_INNER) * jax.nn.silu(z)
    y = rmsnorm(y.reshape(b, s, G_M, D_INNER // G_M), norm_g.reshape(G_M, D_INNER // G_M))
    return y.reshape(b, s, D_INNER) @ w_o


def diff_heads(qd, kd, vd, rope):
    b, s = qd.shape[:2]
    q = qd.reshape(b, s, 2, H_A, DA)
    k = kd.reshape(b, s, 2, H_A, DA)
    v = vd.reshape(b, s, H_A, DV_A)
    q1, q2, k1, k2 = q[:, :, 0], q[:, :, 1], k[:, :, 0], k[:, :, 1]
    if rope is not None:
        q1, q2, k1, k2 = (apply_rope(t, rope) for t in (q1, q2, k1, k2))
    return q1, q2, k1, k2, v


def diff_out(o, subln_g, lam_init, w_o):
    b, s = o.shape[:2]
    o = rmsnorm(o, subln_g) * (1.0 - lam_init)
    return o.reshape(b, s, W_A) @ w_o


def mla_q(cq, q_norm_g, w_uq, rope):
    b, s = cq.shape[:2]
    q = (rmsnorm(cq, q_norm_g) @ w_uq).reshape(b, s, H_C, D_NOPE + D_ROPE)
    q_nope, q_rope = q[..., :D_NOPE], q[..., D_NOPE:]
    if rope is not None:
        q_rope = apply_rope(q_rope, rope)
    return jnp.concatenate([q_nope, q_rope], axis=-1)


def mla_kv(ckv, kr, kv_norm_g, w_ukv, rope):
    b, s = ckv.shape[:2]
    kv = (rmsnorm(ckv, kv_norm_g) @ w_ukv).reshape(b, s, H_C, D_NOPE + D_VC)
    k_nope, v = kv[..., :D_NOPE], kv[..., D_NOPE:]
    kr = kr[:, :, None, :]
    if rope is not None:
        kr = apply_rope(kr, rope)
    k = jnp.concatenate([k_nope, jnp.broadcast_to(kr, (b, s, H_C, D_ROPE))], axis=-1)
    return k, v


def merge_and_mlp(s, gates_raw, ys, gate1, shift2, scale2, gate2, norm2_g, w_out, w1, w2):
    g = jnp.split(jax.nn.sigmoid(gates_raw), N_BRANCH, axis=-1)
    merged = g[0] * ys[0] + g[1] * ys[1] + g[2] * ys[2]
    s = s + gate1 * (merged @ w_out)
    h = modulate(s, norm2_g, shift2, scale2)
    return s + gate2 * (jnp.square(jax.nn.relu(h @ w1)) @ w2)


def setup_inputs(seed: int = 0) -> dict:
    key = jax.random.key(seed)
    k = jax.random.split(key, 32)
    L, D = DEPTH, D_MODEL
    f32 = jnp.float32
    nrm = lambda kk, shape, scale: jax.random.normal(kk, shape, f32) * scale
    gain = lambda kk, shape: 1.0 + nrm(kk, shape, 0.02)
    dt0 = jnp.exp(jax.random.uniform(k[14], (L, 2, H_M), f32, math.log(1e-3), math.log(1e-1)))
    dt_bias = dt0 + jnp.log(-jnp.expm1(-dt0))
    a_log = jnp.log(jax.random.uniform(k[15], (L, 2, H_M), f32, 1.0, 16.0))
    return {
        "x": nrm(k[0], (BATCH, SEQ, D), 1.0),
        "c": nrm(k[1], (BATCH, D), 1.0),
        "ctx": nrm(k[2], (BATCH, CTX_LEN, D), 1.0),
        "c_ctx": nrm(k[3], (D,), 1.0),
        "norm1_g": gain(k[4], (L, D)),
        "norm2_g": gain(k[5], (L, D)),
        "w_ada": nrm(k[6], (L, D, N_MOD * D), D ** -0.5),
        "b_ada": nrm(k[7], (L, N_MOD * D), 0.02),
        "w_in": nrm(k[8], (L, D, P_IN), D ** -0.5),
        "lam_qk": nrm(k[9], (L, 4, DA), 0.1),
        "subln_g": gain(k[10], (L, DV_A)),
        "w_o_diff": nrm(k[11], (L, W_A, D), W_A ** -0.5),
        "conv_w": nrm(k[12], (L, D_CONV, CONV_CH), D_CONV ** -0.5),
        "conv_b": nrm(k[13], (L, CONV_CH), 0.02),
        "a_log": a_log,
        "dt_bias": dt_bias,
        "d_skip": gain(k[16], (L, H_M)),
        "ssm_norm_g": gain(k[17], (L, D_INNER)),
        "w_o_ssm": nrm(k[18], (L, D_INNER, D), D_INNER ** -0.5),
        "q_norm_g": gain(k[19], (L, Q_LORA)),
        "w_uq": nrm(k[20], (L, Q_LORA, H_C * (D_NOPE + D_ROPE)), Q_LORA ** -0.5),
        "kv_norm_g": gain(k[21], (L, KV_LORA)),
        "w_ukv": nrm(k[22], (L, KV_LORA, H_C * (D_NOPE + D_VC)), KV_LORA ** -0.5),
        "w_o_mla": nrm(k[23], (L, W_C, D), W_C ** -0.5),
        "w_out": nrm(k[24], (L, D, D), D ** -0.5),
        "w_mlp1": nrm(k[25], (L, D, D_FF), D ** -0.5),
        "w_mlp2": nrm(k[26], (L, D_FF, D), D_FF ** -0.5),
        "final_norm_g": gain(k[27], (D,)),
    }


def reference(x, c, ctx, c_ctx, norm1_g, norm2_g, w_ada, b_ada, w_in, lam_qk, subln_g,
              w_o_diff, conv_w, conv_b, a_log, dt_bias, d_skip, ssm_norm_g, w_o_ssm,
              q_norm_g, w_uq, kv_norm_g, w_ukv, w_o_mla, w_out, w_mlp1, w_mlp2,
              final_norm_g):
    b, n, _ = x.shape
    rows = n // GRID_W
    rope_a = axial_rope_tables(rows, DA, x.dtype)
    rope_r = axial_rope_tables(rows, D_ROPE, x.dtype)
    lat, cx = x, ctx
    for l in range(DEPTH):
        last = l == DEPTH - 1
        lam_init = 0.8 - 0.6 * math.exp(-0.3 * l)
        lq = lam_qk[l].astype(jnp.float32)
        lam = jnp.exp(jnp.sum(lq[0] * lq[1])) - jnp.exp(jnp.sum(lq[2] * lq[3])) + lam_init
        A = -jnp.exp(a_log[l].astype(jnp.float32))

        mx = [m[:, None, :] for m in modulation(c, w_ada[l], b_ada[l])]
        mc = modulation(c_ctx, w_ada[l], b_ada[l])

        pc = split_in(modulate(cx, norm1_g[l], mc[0], mc[1]) @ w_in[l])
        px = split_in(modulate(lat, norm1_g[l], mx[0], mx[1]) @ w_in[l])

        cq1, cq2, ck1, ck2, cv = diff_heads(pc[0], pc[1], pc[2], None)
        cmk, cmv = mla_kv(pc[7], pc[8], kv_norm_g[l], w_ukv[l], None)
        cxs, cB, cC, cdtf, cdtb = ssm_inputs(pc[4], pc[5], conv_w[l], conv_b[l], dt_bias[l])
        h0 = jnp.zeros((b, H_M, P_M, N_STATE), jnp.float32)
        cy, hf, hb = bidir_ssd(cxs, cB, cC, cdtf, cdtb, A[0], A[1], h0, h0, not last)

        q1, q2, k1, k2, v = diff_heads(px[0], px[1], px[2], rope_a)
        K1 = jnp.concatenate([ck1, k1], axis=1)
        K2 = jnp.concatenate([ck2, k2], axis=1)
        V = jnp.concatenate([cv, v], axis=1)
        o_a = over_query_blocks(lambda a1, a2: diff_attn_core(a1, a2, K1, K2, V, lam), (q1, q2))

        mq = mla_q(px[6], q_norm_g[l], w_uq[l], rope_r)
        mk, mv = mla_kv(px[7], px[8], kv_norm_g[l], w_ukv[l], rope_r)
        MK = jnp.concatenate([cmk, mk], axis=1)
        MV = jnp.concatenate([cmv, mv], axis=1)
        o_c = over_query_blocks(lambda qq: softmax_attn(qq, MK, MV, MLA_SCALE), (mq,))

        xs, Bm, Cm, dtf, dtb = ssm_inputs(px[4], px[5], conv_w[l], conv_b[l], dt_bias[l])
        y, _, _ = bidir_ssd(xs, Bm, Cm, dtf, dtb, A[0], A[1], hf, hb, True)

        ys_x = (diff_out(o_a, subln_g[l], lam_init, w_o_diff[l]),
                ssm_out(y, xs, px[3], d_skip[l], ssm_norm_g[l], w_o_ssm[l]),
                o_c.reshape(b, n, W_C) @ w_o_mla[l])
        new_lat = merge_and_mlp(lat, px[9], ys_x, mx[2], mx[3], mx[4], mx[5],
                                norm2_g[l], w_out[l], w_mlp1[l], w_mlp2[l])

        if not last:
            t = cx.shape[1]
            cmq = mla_q(pc[6], q_norm_g[l], w_uq[l], None)
            ys_c = (diff_out(diff_attn_core(cq1, cq2, ck1, ck2, cv, lam), subln_g[l], lam_init, w_o_diff[l]),
                    ssm_out(cy, cxs, pc[3], d_skip[l], ssm_norm_g[l], w_o_ssm[l]),
                    softmax_attn(cmq, cmk, cmv, MLA_SCALE).reshape(b, t, W_C) @ w_o_mla[l])
            cx = merge_and_mlp(cx, pc[9], ys_c, mc[2], mc[3], mc[4], mc[5],
                               norm2_g[l], w_out[l], w_mlp1[l], w_mlp2[l])
        lat = new_lat
    return rmsnorm(lat, final_norm_g)
```

```python
import functools
import math

import jax
import jax.numpy as jnp
from jax import lax
from jax.experimental import pallas as pl
from jax.experimental.pallas import tpu as pltpu

D_MODEL = 2048
DEPTH = 2
GRID_W = 64
ROPE_BASE = 10000.0
EPS = 1e-6

H_A = 8
DA = 64
DV_A = 2 * DA
W_A = H_A * DV_A

H_M = 16
P_M = 64
D_INNER = H_M * P_M
N_STATE = 128
G_M = 2
D_CONV = 5
CHUNK = 128
CONV_CH = D_INNER + 2 * G_M * N_STATE

H_C = 8
D_NOPE = 128
D_ROPE = 64
D_VC = 128
Q_LORA = 512
KV_LORA = 256
W_C = H_C * D_VC
MLA_SCALE = (D_NOPE + D_ROPE) ** -0.5

D_FF = 4 * D_MODEL
N_BRANCH = 3
N_MOD = 6

IN_SIZES = (2 * H_A * DA, 2 * H_A * DA, W_A, D_INNER, CONV_CH, 2 * H_M,
            Q_LORA, KV_LORA, D_ROPE, N_BRANCH * D_MODEL)

LANE = 128
LOG2E = 1.4426950408889634
VMEM_LIMIT = 48 * 1024 * 1024

BF16 = jnp.bfloat16
F32 = jnp.float32

OFF_Q, OFF_K, OFF_V, OFF_Z, OFF_XBC = 0, 1024, 2048, 3072, 4096
OFF_DT = OFF_XBC + CONV_CH
OFF_CQ = OFF_DT + LANE
OFF_CKV = OFF_CQ + Q_LORA
OFF_KR = OFF_CKV + KV_LORA
OFF_G = OFF_KR + LANE
P_PAD = OFF_G + N_BRANCH * D_MODEL


def _params(sem):
    return pltpu.CompilerParams(dimension_semantics=sem, vmem_limit_bytes=VMEM_LIMIT)


def _norm_mm_kernel(x_ref, g_ref, sc_ref, sh_ref, w_ref, o_ref, h_ref, *, act):
    @pl.when(pl.program_id(1) == 0)
    def _():
        x = x_ref[...]
        y = x * lax.rsqrt(jnp.mean(x * x, axis=-1, keepdims=True) + EPS)
        h = y * g_ref[...] * (1.0 + sc_ref[...]) + sh_ref[...]
        h_ref[...] = h.astype(BF16)

    acc = jnp.dot(h_ref[...], w_ref[...], preferred_element_type=F32)
    if act == "relu2":
        acc = jnp.square(jnp.maximum(acc, 0.0))
    o_ref[...] = acc.astype(o_ref.dtype)


def norm_matmul(x, g, scale, shift, w, *, act=None, out_dtype=F32, tm=512, tn=512):
    m, k = x.shape
    n = w.shape[1]
    tm, tn = min(tm, m), min(tn, n)
    assert m % tm == 0 and n % tn == 0
    row = pl.BlockSpec((1, k), lambda i, j: (0, 0))
    return pl.pallas_call(
        functools.partial(_norm_mm_kernel, act=act),
        grid=(m // tm, n // tn),
        in_specs=[pl.BlockSpec((tm, k), lambda i, j: (i, 0)), row, row, row,
                  pl.BlockSpec((k, tn), lambda i, j: (0, j))],
        out_specs=pl.BlockSpec((tm, tn), lambda i, j: (i, j)),
        out_shape=jax.ShapeDtypeStruct((m, n), out_dtype),
        scratch_shapes=[pltpu.VMEM((tm, k), BF16)],
        compiler_params=_params(("parallel", "arbitrary")),
        name="norm_matmul",
    )(x, g.reshape(1, k), scale.reshape(1, k), shift.reshape(1, k), w)


def _mm_kernel(a_ref, w_ref, res_ref, gate_ref, bias_ref, o_ref, acc_ref, *, nk):
    kk = pl.program_id(2)

    @pl.when(kk == 0)
    def _():
        acc_ref[...] = jnp.zeros_like(acc_ref)

    acc_ref[...] += jnp.dot(a_ref[...].astype(BF16), w_ref[...].astype(BF16),
                            preferred_element_type=F32)

    @pl.when(kk == nk - 1)
    def _():
        o_ref[...] = (res_ref[...] + gate_ref[...] * acc_ref[...] + bias_ref[...]
                      ).astype(o_ref.dtype)


def matmul_epilogue(a, w, res, gate, bias, *, out_dtype=F32, tm=512, tn=1024, tk=2048):
    m, k = a.shape
    n = w.shape[1]
    tm, tn, tk = min(tm, m), min(tn, n), min(tk, k)
    assert m % tm == 0 and n % tn == 0 and k % tk == 0
    nk = k // tk
    return pl.pallas_call(
        functools.partial(_mm_kernel, nk=nk),
        grid=(m // tm, n // tn, nk),
        in_specs=[pl.BlockSpec((tm, tk), lambda i, j, kk: (i, kk)),
                  pl.BlockSpec((tk, tn), lambda i, j, kk: (kk, j)),
                  pl.BlockSpec((tm, tn), lambda i, j, kk: (i, j)),
                  pl.BlockSpec((1, tn), lambda i, j, kk: (0, j)),
                  pl.BlockSpec((1, tn), lambda i, j, kk: (0, j))],
        out_specs=pl.BlockSpec((tm, tn), lambda i, j, kk: (i, j)),
        out_shape=jax.ShapeDtypeStruct((m, n), out_dtype),
        scratch_shapes=[pltpu.VMEM((tm, tn), F32)],
        compiler_params=_params(("parallel", "parallel", "arbitrary")),
        name="matmul_epilogue",
    )(a, w, res, gate.reshape(1, n), bias.reshape(1, n))


def _flash_kernel(q_ref, k_ref, v_ref, o_ref, m_ref, l_ref, acc_ref, *, nk):
    j = pl.program_id(2)

    @pl.when(j == 0)
    def _():
        m_ref[...] = jnp.full_like(m_ref, -jnp.inf)
        l_ref[...] = jnp.zeros_like(l_ref)
        acc_ref[...] = jnp.zeros_like(acc_ref)

    s = lax.dot_general(q_ref[...], k_ref[...], (((1,), (1,)), ((), ())),
                        preferred_element_type=F32)
    m_prev = m_ref[...]
    m_new = jnp.maximum(m_prev, jnp.max(s, axis=-1, keepdims=True))
    alpha = jnp.exp2(m_prev - m_new)
    p = jnp.exp2(s - m_new)
    l_ref[...] = alpha * l_ref[...] + jnp.sum(p, axis=-1, keepdims=True)
    acc_ref[...] = alpha * acc_ref[...] + jnp.dot(
        p.astype(BF16), v_ref[...], preferred_element_type=F32)
    m_ref[...] = m_new

    @pl.when(j == nk - 1)
    def _():
        o_ref[...] = (acc_ref[...] / l_ref[...]).astype(o_ref.dtype)


def flash_attention(q, k, v, *, heads, dk, dv, tq, tk):
    s_len, t_len = q.shape[0], k.shape[0]
    tq, tk = min(tq, s_len), min(tk, t_len)
    assert s_len % tq == 0 and t_len % tk == 0
    nk = t_len // tk
    return pl.pallas_call(
        functools.partial(_flash_kernel, nk=nk),
        grid=(heads, s_len // tq, nk),
        in_specs=[pl.BlockSpec((tq, dk), lambda h, i, j: (i, h)),
                  pl.BlockSpec((tk, dk), lambda h, i, j: (j, h)),
                  pl.BlockSpec((tk, dv), lambda h, i, j: (j, h))],
        out_specs=pl.BlockSpec((tq, dv), lambda h, i, j: (i, h)),
        out_shape=jax.ShapeDtypeStruct((s_len, heads * dv), F32),
        scratch_shapes=[pltpu.VMEM((tq, 1), F32), pltpu.VMEM((tq, 1), F32),
                        pltpu.VMEM((tq, dv), F32)],
        compiler_params=_params(("parallel", "parallel", "arbitrary")),
        name="flash_attention",
    )(q, k, v)


def _split3(a):
    a1 = a.astype(BF16)
    r1 = a - a1.astype(F32)
    a2 = r1.astype(BF16)
    a3 = (r1 - a2.astype(F32)).astype(BF16)
    return a1, a2, a3


def _ssd_kernel(xs_ref, b_ref, c_ref, aexp_ref, dtexp_ref, at_ref, h0_ref,
                y_ref, hfin_ref, h_scr, *, nc):
    d = pl.program_id(0)
    c = pl.program_id(1)

    @pl.when(c == 0)
    def _():
        h_scr[...] = h0_ref[0]

    row = lax.broadcasted_iota(jnp.int32, (CHUNK, CHUNK), 0)
    col = lax.broadcasted_iota(jnp.int32, (CHUNK, CHUNK), 1)
    mask = jnp.where(d == 0, row - col, col - row) >= 0
    tc = mask.astype(BF16)

    a_exp = aexp_ref[0]
    cum = sum(jnp.dot(tc, piece, preferred_element_type=F32) for piece in _split3(a_exp))
    total = jnp.sum(a_exp, axis=0, keepdims=True)
    cum_t = sum(lax.dot_general(piece, tc, (((1,), (1,)), ((), ())),
                                preferred_element_type=F32)
                for piece in _split3(at_ref[0]))

    xd = xs_ref[...] * dtexp_ref[0]
    xdd = (xd * jnp.exp(total - cum)).astype(BF16)
    xd_b = xd.astype(BF16)
    h_prev = h_scr[...]
    h_prev_b = h_prev.astype(BF16)
    lane = lax.broadcasted_iota(jnp.int32, (CHUNK, LANE), 1)
    gw = D_INNER // G_M

    y_parts, st_parts = [], []
    for g in range(G_M):
        bg = b_ref[:, g * N_STATE:(g + 1) * N_STATE]
        cg = c_ref[:, g * N_STATE:(g + 1) * N_STATE].astype(BF16)
        cb = lax.dot_general(cg, bg.astype(BF16), (((1,), (1,)), ((), ())),
                             preferred_element_type=F32)
        y_off = jnp.dot(cg, h_prev_b[:, g * gw:(g + 1) * gw], preferred_element_type=F32)
        st_parts.append(jnp.dot(bg.T.astype(BF16), xdd[:, g * gw:(g + 1) * gw],
                                preferred_element_type=F32))
        for pr in range(gw // LANE):
            lo = g * gw + pr * LANE
            xp = xd_b[:, lo:lo + LANE]
            halves = (jnp.where(lane < P_M, xp, 0), jnp.where(lane >= P_M, xp, 0))
            y_pair = y_off[:, pr * LANE:(pr + 1) * LANE] * jnp.exp(cum[:, lo:lo + LANE])
            for half in range(2):
                hh = lo // P_M + half
                seg = cum[:, hh * P_M:hh * P_M + 1] - cum_t[hh:hh + 1, :]
                mm = (jnp.exp(jnp.where(mask, seg, -jnp.inf)) * cb).astype(BF16)
                y_pair += jnp.dot(mm, halves[half], preferred_element_type=F32)
            y_parts.append(y_pair)

    y_ref[0] = jnp.concatenate(y_parts, axis=1)
    h_new = h_prev * jnp.exp(total) + jnp.concatenate(st_parts, axis=1)
    h_scr[...] = h_new

    @pl.when(c == nc - 1)
    def _():
        hfin_ref[0] = h_new


def ssd_scan(xs, bm, cm, a_exp, dt_exp, a_t, h0):
    s_len = xs.shape[0]
    nc = s_len // CHUNK
    cidx = lambda d, c: c + d * (nc - 1 - 2 * c)
    return pl.pallas_call(
        functools.partial(_ssd_kernel, nc=nc),
        grid=(2, nc),
        in_specs=[pl.BlockSpec((CHUNK, D_INNER), lambda d, c: (cidx(d, c), 0)),
                  pl.BlockSpec((CHUNK, G_M * N_STATE), lambda d, c: (cidx(d, c), 0)),
                  pl.BlockSpec((CHUNK, G_M * N_STATE), lambda d, c: (cidx(d, c), 0)),
                  pl.BlockSpec((1, CHUNK, D_INNER), lambda d, c: (d, cidx(d, c), 0)),
                  pl.BlockSpec((1, CHUNK, D_INNER), lambda d, c: (d, cidx(d, c), 0)),
                  pl.BlockSpec((1, H_M, CHUNK), lambda d, c: (d, 0, cidx(d, c))),
                  pl.BlockSpec((1, N_STATE, D_INNER), lambda d, c: (d, 0, 0))],
        out_specs=[pl.BlockSpec((1, CHUNK, D_INNER), lambda d, c: (d, cidx(d, c), 0)),
                   pl.BlockSpec((1, N_STATE, D_INNER), lambda d, c: (d, 0, 0))],
        out_shape=[jax.ShapeDtypeStruct((2, s_len, D_INNER), F32),
                   jax.ShapeDtypeStruct((2, N_STATE, D_INNER), F32)],
        scratch_shapes=[pltpu.VMEM((N_STATE, D_INNER), F32)],
        compiler_params=_params(("arbitrary", "arbitrary")),
        name="ssd_scan",
    )(xs, bm, cm, a_exp, dt_exp, a_t, h0)


def _merge_kernel(lamqk_ref, o1_ref, o2_ref, subg_ref, y_ref, xs_ref, dsk_ref, z_ref,
                  sng_ref, oc_ref, w0_ref, w1_ref, w2_ref, g0_ref, g1_ref, g2_ref,
                  o_ref, br_ref, *, lam_init):
    @pl.when(pl.program_id(1) == 0)
    def _():
        lq = lamqk_ref[...]
        lam = (jnp.exp(jnp.sum(lq[0:1] * lq[1:2], axis=-1, keepdims=True))
               - jnp.exp(jnp.sum(lq[2:3] * lq[3:4], axis=-1, keepdims=True)) + lam_init)
        for h in range(H_A):
            sl = slice(h * DV_A, (h + 1) * DV_A)
            o = o1_ref[:, sl] - lam * o2_ref[:, sl]
            o = o * lax.rsqrt(jnp.mean(o * o, axis=-1, keepdims=True) + EPS)
            br_ref[0, :, sl] = (o * subg_ref[...] * (1.0 - lam_init)).astype(BF16)
        z = z_ref[...]
        u = (y_ref[0] + y_ref[1] + xs_ref[...] * dsk_ref[...]) * (z * jax.nn.sigmoid(z))
        gw = D_INNER // G_M
        for g in range(G_M):
            sl = slice(g * gw, (g + 1) * gw)
            ug = u[:, sl]
            ug = ug * lax.rsqrt(jnp.mean(ug * ug, axis=-1, keepdims=True) + EPS)
            br_ref[1, :, sl] = (ug * sng_ref[:, sl]).astype(BF16)
        br_ref[2] = oc_ref[...].astype(BF16)

    acc = jax.nn.sigmoid(g0_ref[...]) * jnp.dot(br_ref[0], w0_ref[...],
                                                preferred_element_type=F32)
    acc += jax.nn.sigmoid(g1_ref[...]) * jnp.dot(br_ref[1], w1_ref[...],
                                                 preferred_element_type=F32)
    acc += jax.nn.sigmoid(g2_ref[...]) * jnp.dot(br_ref[2], w2_ref[...],
                                                 preferred_element_type=F32)
    o_ref[...] = acc.astype(o_ref.dtype)


def merge_branches(lam_qk, o1, o2, subln_g, y, xs, dskip_exp, ssm_norm_g, oc,
                   w0, w1, w2, proj, *, lam_init, tm=256, tn=512):
    m = o1.shape[0]
    n = D_MODEL
    tm = min(tm, m)
    assert m % tm == 0 and n % tn == 0 and OFF_G % tn == 0 and OFF_Z % D_INNER == 0
    wide = lambda i, j: (i, 0)
    fixed = lambda i, j: (0, 0)
    wspec = pl.BlockSpec((W_A, tn), lambda i, j: (0, j))
    gspec = lambda b: pl.BlockSpec((tm, tn), lambda i, j: (i, (OFF_G + b * n) // tn + j))
    return pl.pallas_call(
        functools.partial(_merge_kernel, lam_init=lam_init),
        grid=(m // tm, n // tn),
        in_specs=[pl.BlockSpec((4, DA), fixed),
                  pl.BlockSpec((tm, W_A), wide), pl.BlockSpec((tm, W_A), wide),
                  pl.BlockSpec((1, DV_A), fixed),
                  pl.BlockSpec((2, tm, D_INNER), lambda i, j: (0, i, 0)),
                  pl.BlockSpec((tm, D_INNER), wide),
                  pl.BlockSpec((1, D_INNER), fixed),
                  pl.BlockSpec((tm, D_INNER), lambda i, j: (i, OFF_Z // D_INNER)),
                  pl.BlockSpec((1, D_INNER), fixed),
                  pl.BlockSpec((tm, W_C), wide),
                  wspec, wspec, wspec, gspec(0), gspec(1), gspec(2)],
        out_specs=pl.BlockSpec((tm, tn), lambda i, j: (i, j)),
        out_shape=jax.ShapeDtypeStruct((m, n), BF16),
        scratch_shapes=[pltpu.VMEM((N_BRANCH, tm, W_A), BF16)],
        compiler_params=_params(("parallel", "arbitrary")),
        name="merge_branches",
    )(lam_qk, o1, o2, subln_g.reshape(1, DV_A), y, xs, dskip_exp.reshape(1, D_INNER),
      proj, ssm_norm_g.reshape(1, D_INNER), oc, w0, w1, w2, proj, proj, proj)


def _rmsnorm_kernel(x_ref, g_ref, o_ref):
    x = x_ref[...]
    o_ref[...] = x * lax.rsqrt(jnp.mean(x * x, axis=-1, keepdims=True) + EPS) * g_ref[...]


def rmsnorm_rows(x, g, *, tm=512):
    m, k = x.shape
    tm = min(tm, m)
    return pl.pallas_call(
        _rmsnorm_kernel,
        grid=(m // tm,),
        in_specs=[pl.BlockSpec((tm, k), lambda i: (i, 0)), pl.BlockSpec((1, k), lambda i: (0, 0))],
        out_specs=pl.BlockSpec((tm, k), lambda i: (i, 0)),
        out_shape=jax.ShapeDtypeStruct((m, k), F32),
        compiler_params=_params(("parallel",)),
        name="final_rmsnorm",
    )(x, g.reshape(1, k))


def _rope_tables(rows, dim):
    row = jnp.repeat(jnp.arange(rows), GRID_W).astype(F32)
    col = jnp.tile(jnp.arange(GRID_W), rows).astype(F32)
    n_freq = dim // 4
    inv = 1.0 / (ROPE_BASE ** (jnp.arange(n_freq, dtype=F32) / n_freq))
    ang = jnp.concatenate([row[:, None] * inv, col[:, None] * inv], axis=-1)
    return jnp.cos(ang), jnp.sin(ang)


def _rope(t, tables):
    cos, sin = tables
    half = t.shape[-1] // 2
    t1, t2 = t[..., :half], t[..., half:]
    cc, ss = cos[:, None, :], sin[:, None, :]
    return jnp.concatenate([t1 * cc - t2 * ss, t2 * cc + t1 * ss], axis=-1)


def _pad_last(t, width):
    return jnp.pad(t, [(0, 0)] * (t.ndim - 1) + [(0, width - t.shape[-1])])


def _layout_w_in(w):
    offs, acc = [], 0
    for sz in IN_SIZES[:-1]:
        acc += sz
        offs.append(acc)
    parts = jnp.split(w, offs, axis=-1)
    parts[5] = _pad_last(parts[5], LANE)
    parts[8] = _pad_last(parts[8], LANE)
    return jnp.concatenate(parts, axis=-1).astype(BF16)


def _layout_w_uq(w):
    w = w.reshape(Q_LORA, H_C, D_NOPE + D_ROPE)
    return _pad_last(w, 2 * LANE).reshape(Q_LORA, H_C * 2 * LANE).astype(BF16)


def _layout_w_ukv(w):
    w = w.reshape(KV_LORA, H_C, D_NOPE + D_VC)
    return jnp.concatenate([w[..., :D_NOPE].reshape(KV_LORA, H_C * D_NOPE),
                            w[..., D_NOPE:].reshape(KV_LORA, H_C * D_VC)], axis=-1).astype(BF16)


def _dwconv_silu(u, w, bias):
    pad = D_CONV // 2
    up = jnp.pad(u, ((pad, pad), (0, 0)))
    y = sum(up[i:i + u.shape[0]] * w[i] for i in range(D_CONV)) + bias
    return y * jax.nn.sigmoid(y)


def _ssm_operands(proj, conv_w, conv_b, dt_bias, a_neg):
    xbc = _dwconv_silu(proj[:, OFF_XBC:OFF_XBC + CONV_CH], conv_w, conv_b)
    xs = xbc[:, :D_INNER]
    bm = xbc[:, D_INNER:D_INNER + G_M * N_STATE]
    cm = xbc[:, D_INNER + G_M * N_STATE:]
    dt_raw = proj[:, OFF_DT:OFF_DT + 2 * H_M].reshape(-1, 2, H_M)
    dt = jnp.moveaxis(jax.nn.softplus(dt_raw + dt_bias[None]), 1, 0)
    a = dt * a_neg[:, None, :]
    return (xs, bm, cm, jnp.repeat(a, P_M, axis=-1), jnp.repeat(dt, P_M, axis=-1),
            jnp.swapaxes(a, 1, 2))


def _diff_qk(proj, off, tables, scale):
    t = proj[:, off:off + 2 * H_A * DA].reshape(-1, 2, H_A, DA)
    outs = []
    for mp in range(2):
        u = t[:, mp]
        if tables is not None:
            u = _rope(u, tables)
        outs.append(_pad_last(u * scale, LANE).reshape(-1, H_A * LANE).astype(BF16))
    return outs


def _mla_q(qfull, tables, scale):
    q = qfull.reshape(-1, H_C, 2 * LANE)
    if tables is not None:
        q = jnp.concatenate([q[..., :D_NOPE], _rope(q[..., D_NOPE:D_NOPE + D_ROPE], tables),
                             q[..., D_NOPE + D_ROPE:]], axis=-1)
    return (q * scale).reshape(-1, H_C * 2 * LANE).astype(BF16)


def _mla_k(kvfull, kr, tables):
    s_len = kvfull.shape[0]
    k_nope = kvfull[:, :H_C * D_NOPE].reshape(s_len, H_C, D_NOPE)
    kr = kr[:, None, :]
    if tables is not None:
        kr = _rope(kr, tables)
    kr = jnp.broadcast_to(_pad_last(kr, LANE), (s_len, H_C, LANE))
    return jnp.concatenate([k_nope, kr], axis=-1).reshape(s_len, H_C * 2 * LANE).astype(BF16)


def kernel(x, c, ctx, c_ctx, norm1_g, norm2_g, w_ada, b_ada, w_in, lam_qk, subln_g, w_o_diff,
           conv_w, conv_b, a_log, dt_bias, d_skip, ssm_norm_g, w_o_ssm, q_norm_g, w_uq,
           kv_norm_g, w_ukv, w_o_mla, w_out, w_mlp1, w_mlp2, final_norm_g):
    _, n, _ = x.shape
    rows = n // GRID_W
    rope_a = _rope_tables(rows, DA)
    rope_r = _rope_tables(rows, D_ROPE)
    lat, cx = x[0], ctx[0]
    t_ctx = cx.shape[0]
    sc_a = (DA ** -0.5) * LOG2E
    sc_c = MLA_SCALE * LOG2E
    zero_d = jnp.zeros((D_MODEL,), F32)

    for l in range(DEPTH):
        last = l == DEPTH - 1
        lam_init = 0.8 - 0.6 * math.exp(-0.3 * l)
        a_neg = -jnp.exp(a_log[l].astype(F32))

        cond = jnp.concatenate([c, c_ctx[None], jnp.zeros((6, D_MODEL), F32)], axis=0)
        cond = cond * jax.nn.sigmoid(cond)
        nmod = N_MOD * D_MODEL
        mod = matmul_epilogue(cond, w_ada[l], jnp.zeros((8, nmod), F32), jnp.ones((nmod,), F32),
                              b_ada[l], tm=8, tn=1024, tk=D_MODEL)
        mx = [mod[0, i * D_MODEL:(i + 1) * D_MODEL] for i in range(N_MOD)]
        mc = [mod[1, i * D_MODEL:(i + 1) * D_MODEL] for i in range(N_MOD)]

        w_in_l = _layout_w_in(w_in[l])
        w_uq_l, w_ukv_l = _layout_w_uq(w_uq[l]), _layout_w_ukv(w_ukv[l])
        w_od, w_os, w_om = (w.astype(BF16) for w in (w_o_diff[l], w_o_ssm[l], w_o_mla[l]))
        w_out_l, w1_l, w2_l = (w.astype(BF16) for w in (w_out[l], w_mlp1[l], w_mlp2[l]))
        dskip_exp = jnp.repeat(d_skip[l], P_M)
        zq, zkv = jnp.zeros((Q_LORA,), F32), jnp.zeros((KV_LORA,), F32)

        pc = norm_matmul(cx, norm1_g[l], mc[1], mc[0], w_in_l, tm=256, tn=1280)
        px = norm_matmul(lat, norm1_g[l], mx[1], mx[0], w_in_l, tm=512, tn=1280)

        cq1, cq2 = _diff_qk(pc, OFF_Q, None, sc_a)
        ck1, ck2 = _diff_qk(pc, OFF_K, None, 1.0)
        ckv_up = norm_matmul(pc[:, OFF_CKV:OFF_CKV + KV_LORA], kv_norm_g[l], zkv, zkv, w_ukv_l)
        cmk = _mla_k(ckv_up, pc[:, OFF_KR:OFF_KR + D_ROPE], None)
        c_ssm = _ssm_operands(pc, conv_w[l], conv_b[l], dt_bias[l], a_neg)
        h0 = jnp.zeros((2, N_STATE, D_INNER), F32)
        cy, hfin = ssd_scan(*c_ssm, h0)

        q1, q2 = _diff_qk(px, OFF_Q, rope_a, sc_a)
        k1, k2 = _diff_qk(px, OFF_K, rope_a, 1.0)
        kk1 = jnp.concatenate([ck1, k1], axis=0)
        kk2 = jnp.concatenate([ck2, k2], axis=0)
        vv = jnp.concatenate([pc[:, OFF_V:OFF_V + W_A], px[:, OFF_V:OFF_V + W_A]], axis=0).astype(BF16)
        o1 = flash_attention(q1, kk1, vv, heads=H_A, dk=LANE, dv=DV_A, tq=1024, tk=1280)
        o2 = flash_attention(q2, kk2, vv, heads=H_A, dk=LANE, dv=DV_A, tq=1024, tk=1280)

        q_up = norm_matmul(px[:, OFF_CQ:OFF_CQ + Q_LORA], q_norm_g[l], zq, zq, w_uq_l)
        kv_up = norm_matmul(px[:, OFF_CKV:OFF_CKV + KV_LORA], kv_norm_g[l], zkv, zkv, w_ukv_l)
        mq = _mla_q(q_up, rope_r, sc_c)
        mk = jnp.concatenate([cmk, _mla_k(kv_up, px[:, OFF_KR:OFF_KR + D_ROPE], rope_r)], axis=0)
        mv = jnp.concatenate([ckv_up[:, H_C * D_NOPE:], kv_up[:, H_C * D_NOPE:]], axis=0).astype(BF16)
        oc = flash_attention(mq, mk, mv, heads=H_C, dk=2 * LANE, dv=D_VC, tq=1024, tk=1280)

        x_ssm = _ssm_operands(px, conv_w[l], conv_b[l], dt_bias[l], a_neg)
        y, _ = ssd_scan(*x_ssm, hfin)

        merged = merge_branches(lam_qk[l], o1, o2, subln_g[l], y, x_ssm[0], dskip_exp,
                                ssm_norm_g[l], oc, w_od, w_os, w_om, px, lam_init=lam_init)
        s1 = matmul_epilogue(merged, w_out_l, lat, mx[2], zero_d, tm=512, tn=1024, tk=D_MODEL)
        hmid = norm_matmul(s1, norm2_g[l], mx[4], mx[3], w1_l, act="relu2", out_dtype=BF16,
                           tm=512, tn=1024)
        new_lat = matmul_epilogue(hmid, w2_l, s1, mx[5], zero_d, tm=512, tn=1024, tk=2048)

        if not last:
            co1 = flash_attention(cq1, ck1, vv[:t_ctx], heads=H_A, dk=LANE, dv=DV_A, tq=256, tk=256)
            co2 = flash_attention(cq2, ck2, vv[:t_ctx], heads=H_A, dk=LANE, dv=DV_A, tq=256, tk=256)
            cq_up = norm_matmul(pc[:, OFF_CQ:OFF_CQ + Q_LORA], q_norm_g[l], zq, zq, w_uq_l)
            coc = flash_attention(_mla_q(cq_up, None, sc_c), cmk, mv[:t_ctx], heads=H_C,
                                  dk=2 * LANE, dv=D_VC, tq=256, tk=256)
            cmerged = merge_branches(lam_qk[l], co1, co2, subln_g[l], cy, c_ssm[0], dskip_exp,
                                     ssm_norm_g[l], coc, w_od, w_os, w_om, pc, lam_init=lam_init)
            cs1 = matmul_epilogue(cmerged, w_out_l, cx, mc[2], zero_d, tm=256, tn=1024, tk=D_MODEL)
            chid = norm_matmul(cs1, norm2_g[l], mc[4], mc[3], w1_l, act="relu2", out_dtype=BF16,
                               tm=256, tn=1024)
            cx = matmul_epilogue(chid, w2_l, cs1, mc[5], zero_d, tm=256, tn=1024, tk=2048)
        lat = new_lat

    return rmsnorm_rows(lat, final_norm_g)[None]
```

```python
import functools
import math

import jax
import jax.numpy as jnp
from jax import lax
from jax.experimental import pallas as pl
from jax.experimental.pallas import tpu as pltpu

D_MODEL = 2048
DEPTH = 2
GRID_W = 64
ROPE_BASE = 10000.0
EPS = 1e-6

H_A = 8
DA = 64
DV_A = 2 * DA
W_A = H_A * DV_A

H_M = 16
P_M = 64
D_INNER = H_M * P_M
N_STATE = 128
G_M = 2
D_CONV = 5
CHUNK = 128
CONV_CH = D_INNER + 2 * G_M * N_STATE

H_C = 8
D_NOPE = 128
D_ROPE = 64
D_VC = 128
Q_LORA = 512
KV_LORA = 256
W_C = H_C * D_VC
MLA_SCALE = (D_NOPE + D_ROPE) ** -0.5

D_FF = 4 * D_MODEL
N_BRANCH = 3
N_MOD = 6

IN_SIZES = (2 * H_A * DA, 2 * H_A * DA, W_A, D_INNER, CONV_CH, 2 * H_M,
            Q_LORA, KV_LORA, D_ROPE, N_BRANCH * D_MODEL)

LANE = 128
LOG2E = 1.4426950408889634
VMEM_LIMIT = 48 * 1024 * 1024

BF16 = jnp.bfloat16
F32 = jnp.float32

OFF_Q, OFF_K, OFF_V, OFF_Z, OFF_XBC = 0, 1024, 2048, 3072, 4096
OFF_DT = OFF_XBC + CONV_CH
OFF_CQ = OFF_DT + LANE
OFF_CKV = OFF_CQ + Q_LORA
OFF_KR = OFF_CKV + KV_LORA
OFF_G = OFF_KR + LANE
P_PAD = OFF_G + N_BRANCH * D_MODEL


def _params(sem):
    return pltpu.CompilerParams(dimension_semantics=sem, vmem_limit_bytes=VMEM_LIMIT)


def _norm_mm_kernel(x_ref, g_ref, sc_ref, sh_ref, w_ref, o_ref, h_ref, *, act):
    @pl.when(pl.program_id(1) == 0)
    def _():
        x = x_ref[...]
        y = x * lax.rsqrt(jnp.mean(x * x, axis=-1, keepdims=True) + EPS)
        h = y * g_ref[...] * (1.0 + sc_ref[...]) + sh_ref[...]
        h_ref[...] = h.astype(BF16)

    acc = jnp.dot(h_ref[...], w_ref[...], preferred_element_type=F32)
    if act == "relu2":
        acc = jnp.square(jnp.maximum(acc, 0.0))
    o_ref[...] = acc.astype(o_ref.dtype)


def norm_matmul(x, g, scale, shift, w, *, act=None, out_dtype=F32, tm=512, tn=512):
    m, k = x.shape
    n = w.shape[1]
    tm, tn = min(tm, m), min(tn, n)
    assert m % tm == 0 and n % tn == 0
    row = pl.BlockSpec((1, k), lambda i, j: (0, 0))
    return pl.pallas_call(
        functools.partial(_norm_mm_kernel, act=act),
        grid=(m // tm, n // tn),
        in_specs=[pl.BlockSpec((tm, k), lambda i, j: (i, 0)), row, row, row,
                  pl.BlockSpec((k, tn), lambda i, j: (0, j))],
        out_specs=pl.BlockSpec((tm, tn), lambda i, j: (i, j)),
        out_shape=jax.ShapeDtypeStruct((m, n), out_dtype),
        scratch_shapes=[pltpu.VMEM((tm, k), BF16)],
        compiler_params=_params(("parallel", "arbitrary")),
        name="norm_matmul",
    )(x, g.reshape(1, k), scale.reshape(1, k), shift.reshape(1, k), w)


def _mm_kernel(a_ref, w_ref, res_ref, gate_ref, bias_ref, o_ref, acc_ref, *, nk):
    kk = pl.program_id(2)

    @pl.when(kk == 0)
    def _():
        acc_ref[...] = jnp.zeros_like(acc_ref)

    acc_ref[...] += jnp.dot(a_ref[...].astype(BF16), w_ref[...].astype(BF16),
                            preferred_element_type=F32)

    @pl.when(kk == nk - 1)
    def _():
        o_ref[...] = (res_ref[...] + gate_ref[...] * acc_ref[...] + bias_ref[...]
                      ).astype(o_ref.dtype)


def matmul_epilogue(a, w, res, gate, bias, *, out_dtype=F32, tm=512, tn=1024, tk=2048):
    m, k = a.shape
    n = w.shape[1]
    tm, tn, tk = min(tm, m), min(tn, n), min(tk, k)
    assert m % tm == 0 and n % tn == 0 and k % tk == 0
    nk = k // tk
    return pl.pallas_call(
        functools.partial(_mm_kernel, nk=nk),
        grid=(m // tm, n // tn, nk),
        in_specs=[pl.BlockSpec((tm, tk), lambda i, j, kk: (i, kk)),
                  pl.BlockSpec((tk, tn), lambda i, j, kk: (kk, j)),
                  pl.BlockSpec((tm, tn), lambda i, j, kk: (i, j)),
                  pl.BlockSpec((1, tn), lambda i, j, kk: (0, j)),
                  pl.BlockSpec((1, tn), lambda i, j, kk: (0, j))],
        out_specs=pl.BlockSpec((tm, tn), lambda i, j, kk: (i, j)),
        out_shape=jax.ShapeDtypeStruct((m, n), out_dtype),
        scratch_shapes=[pltpu.VMEM((tm, tn), F32)],
        compiler_params=_params(("parallel", "parallel", "arbitrary")),
        name="matmul_epilogue",
    )(a, w, res, gate.reshape(1, n), bias.reshape(1, n))


def _flash_init(m_ref, l_ref, acc_ref):
    m_ref[...] = jnp.full_like(m_ref, -jnp.inf)
    l_ref[...] = jnp.zeros_like(l_ref)
    acc_ref[...] = jnp.zeros_like(acc_ref)


def _flash_streams(streams, s_ref, p_ref, m_ref, l_ref, acc_ref):
    n = len(streams)
    alpha = [None] * n
    vts = [None] * n
    for t in range(n + 2):
        if t < n:
            kx, qx, vts[t] = streams[t]()
            s_ref[t % 2] = lax.dot_general(kx, qx, (((1,), (1,)), ((), ())),
                                           preferred_element_type=F32)
        u = t - 1
        if 0 <= u < n:
            m_prev = m_ref[u]
            m_new = jnp.maximum(m_prev, jnp.max(s_ref[u % 2], axis=0, keepdims=True))
            m_ref[u] = m_new
            alpha[u] = jnp.exp2(m_prev - m_new)
            p = jnp.exp2(s_ref[u % 2] - m_new)
            l_ref[u] = alpha[u] * l_ref[u] + jnp.sum(p, axis=0, keepdims=True)
            p_ref[u % 2] = p.astype(BF16)
        w = t - 2
        if 0 <= w < n:
            acc_ref[w] = alpha[w] * acc_ref[w] + jnp.dot(vts[w], p_ref[w % 2],
                                                         preferred_element_type=F32)


def _diff_flash_kernel(lamqk_ref, subg_ref, q1_ref, q2_ref, k1_ref, k2_ref, vt_ref, o_ref,
                       s_ref, p_ref, m_ref, l_ref, acc_ref, *, nk, lam_init):
    j = pl.program_id(2)

    @pl.when(j == 0)
    def _():
        _flash_init(m_ref, l_ref, acc_ref)

    lane = lax.broadcasted_iota(jnp.int32, q1_ref.shape, 1)

    def stream(q_ref, k_ref, half):
        def load():
            q = q_ref[...]
            qh = jnp.where(lane >= DA if half else lane < DA, q, jnp.zeros_like(q))
            return k_ref[...], qh, vt_ref[half * DV_A:(half + 1) * DV_A, :]
        return load

    _flash_streams([stream(q1_ref, k1_ref, 0), stream(q1_ref, k1_ref, 1),
                    stream(q2_ref, k2_ref, 0), stream(q2_ref, k2_ref, 1)],
                   s_ref, p_ref, m_ref, l_ref, acc_ref)

    @pl.when(j == nk - 1)
    def _():
        lq = lamqk_ref[...]
        lam = (jnp.exp(jnp.sum(lq[0:1] * lq[1:2], axis=-1, keepdims=True))
               - jnp.exp(jnp.sum(lq[2:3] * lq[3:4], axis=-1, keepdims=True)) + lam_init)
        for half in range(2):
            o = acc_ref[half] / l_ref[half] - lam * (acc_ref[2 + half] / l_ref[2 + half])
            o = o * lax.rsqrt(jnp.mean(o * o, axis=0, keepdims=True) + EPS)
            o = o * subg_ref[...] * (1.0 - lam_init)
            o_ref[:, half * DV_A:(half + 1) * DV_A] = o.T.astype(o_ref.dtype)


def diff_flash_attention(lam_qk, subln_g, q1, q2, k1, k2, vt, *, lam_init, tq, tk):
    s_len, t_len = q1.shape[0], k1.shape[0]
    tq, tk = min(tq, s_len), min(tk, t_len)
    assert s_len % tq == 0 and t_len % tk == 0
    nk = t_len // tk
    qspec = pl.BlockSpec((tq, LANE), lambda h, i, j: (i, h))
    kspec = pl.BlockSpec((tk, LANE), lambda h, i, j: (j, h))
    fixed = lambda h, i, j: (0, 0)
    return pl.pallas_call(
        functools.partial(_diff_flash_kernel, nk=nk, lam_init=lam_init),
        grid=(H_A // 2, s_len // tq, nk),
        in_specs=[pl.BlockSpec((4, DA), fixed), pl.BlockSpec((DV_A, 1), fixed),
                  qspec, qspec, kspec, kspec,
                  pl.BlockSpec((2 * DV_A, tk), lambda h, i, j: (h, j))],
        out_specs=pl.BlockSpec((tq, 2 * DV_A), lambda h, i, j: (i, h)),
        out_shape=jax.ShapeDtypeStruct((s_len, W_A), BF16),
        scratch_shapes=[pltpu.VMEM((2, tk, tq), F32), pltpu.VMEM((2, tk, tq), BF16),
                        pltpu.VMEM((4, 1, tq), F32), pltpu.VMEM((4, 1, tq), F32),
                        pltpu.VMEM((4, DV_A, tq), F32)],
        compiler_params=_params(("parallel", "parallel", "arbitrary")),
        name="diff_flash",
    )(lam_qk, subln_g.reshape(DV_A, 1), q1, q2, k1, k2, vt)


def _mla_flash_kernel(q_ref, k_ref, vt_ref, o_ref, s_ref, p_ref, m_ref, l_ref, acc_ref,
                      *, nk, hps):
    j = pl.program_id(2)

    @pl.when(j == 0)
    def _():
        _flash_init(m_ref, l_ref, acc_ref)

    dk = 2 * LANE

    def stream(hh):
        return lambda: (k_ref[:, hh * dk:(hh + 1) * dk], q_ref[:, hh * dk:(hh + 1) * dk],
                        vt_ref[hh * D_VC:(hh + 1) * D_VC, :])

    _flash_streams([stream(hh) for hh in range(hps)], s_ref, p_ref, m_ref, l_ref, acc_ref)

    @pl.when(j == nk - 1)
    def _():
        for hh in range(hps):
            o = acc_ref[hh] / l_ref[hh]
            o_ref[:, hh * D_VC:(hh + 1) * D_VC] = o.T.astype(o_ref.dtype)


def mla_flash_attention(q, k, vt, *, tq, tk, hps=4):
    s_len, t_len = q.shape[0], k.shape[0]
    tq, tk = min(tq, s_len), min(tk, t_len)
    assert s_len % tq == 0 and t_len % tk == 0 and H_C % hps == 0
    nk = t_len // tk
    dk = 2 * LANE
    return pl.pallas_call(
        functools.partial(_mla_flash_kernel, nk=nk, hps=hps),
        grid=(H_C // hps, s_len // tq, nk),
        in_specs=[pl.BlockSpec((tq, hps * dk), lambda h, i, j: (i, h)),
                  pl.BlockSpec((tk, hps * dk), lambda h, i, j: (j, h)),
                  pl.BlockSpec((hps * D_VC, tk), lambda h, i, j: (h, j))],
        out_specs=pl.BlockSpec((tq, hps * D_VC), lambda h, i, j: (i, h)),
        out_shape=jax.ShapeDtypeStruct((s_len, W_C), BF16),
        scratch_shapes=[pltpu.VMEM((2, tk, tq), F32), pltpu.VMEM((2, tk, tq), BF16),
                        pltpu.VMEM((hps, 1, tq), F32), pltpu.VMEM((hps, 1, tq), F32),
                        pltpu.VMEM((hps, D_VC, tq), F32)],
        compiler_params=_params(("parallel", "parallel", "arbitrary")),
        name="mla_flash",
    )(q, k, vt)


def _split3(a):
    a1 = a.astype(BF16)
    r1 = a - a1.astype(F32)
    a2 = r1.astype(BF16)
    a3 = (r1 - a2.astype(F32)).astype(BF16)
    return a1, a2, a3


def _ssd_kernel(xs_ref, b_ref, c_ref, aexp_ref, dtexp_ref, at_ref, h0_ref,
                y_ref, hfin_ref, h_scr, *, nc):
    d = pl.program_id(0)
    c = pl.program_id(1)

    @pl.when(c == 0)
    def _():
        h_scr[...] = h0_ref[0]

    row = lax.broadcasted_iota(jnp.int32, (CHUNK, CHUNK), 0)
    col = lax.broadcasted_iota(jnp.int32, (CHUNK, CHUNK), 1)
    mask = jnp.where(d == 0, row - col, col - row) >= 0
    tc = mask.astype(BF16)

    a_exp = aexp_ref[0]
    cum = sum(jnp.dot(tc, piece, preferred_element_type=F32) for piece in _split3(a_exp))
    total = jnp.sum(a_exp, axis=0, keepdims=True)
    cum_t = sum(lax.dot_general(piece, tc, (((1,), (1,)), ((), ())),
                                preferred_element_type=F32)
                for piece in _split3(at_ref[0]))

    xd = xs_ref[...] * dtexp_ref[0]
    xdd = (xd * jnp.exp(total - cum)).astype(BF16)
    xd_b = xd.astype(BF16)
    h_prev = h_scr[...]
    h_prev_b = h_prev.astype(BF16)
    lane = lax.broadcasted_iota(jnp.int32, (CHUNK, LANE), 1)
    gw = D_INNER // G_M

    y_parts, st_parts = [], []
    for g in range(G_M):
        bg = b_ref[:, g * N_STATE:(g + 1) * N_STATE]
        cg = c_ref[:, g * N_STATE:(g + 1) * N_STATE].astype(BF16)
        cb = lax.dot_general(cg, bg.astype(BF16), (((1,), (1,)), ((), ())),
                             preferred_element_type=F32)
        y_off = jnp.dot(cg, h_prev_b[:, g * gw:(g + 1) * gw], preferred_element_type=F32)
        st_parts.append(jnp.dot(bg.T.astype(BF16), xdd[:, g * gw:(g + 1) * gw],
                                preferred_element_type=F32))
        for pr in range(gw // LANE):
            lo = g * gw + pr * LANE
            xp = xd_b[:, lo:lo + LANE]
            halves = (jnp.where(lane < P_M, xp, 0), jnp.where(lane >= P_M, xp, 0))
            y_pair = y_off[:, pr * LANE:(pr + 1) * LANE] * jnp.exp(cum[:, lo:lo + LANE])
            for half in range(2):
                hh = lo // P_M + half
                seg = cum[:, hh * P_M:hh * P_M + 1] - cum_t[hh:hh + 1, :]
                mm = (jnp.exp(jnp.where(mask, seg, -jnp.inf)) * cb).astype(BF16)
                y_pair += jnp.dot(mm, halves[half], preferred_element_type=F32)
            y_parts.append(y_pair)

    y_ref[0] = jnp.concatenate(y_parts, axis=1)
    h_new = h_prev * jnp.exp(total) + jnp.concatenate(st_parts, axis=1)
    h_scr[...] = h_new

    @pl.when(c == nc - 1)
    def _():
        hfin_ref[0] = h_new


def ssd_scan(xs, bm, cm, a_exp, dt_exp, a_t, h0):
    s_len = xs.shape[0]
    nc = s_len // CHUNK
    cidx = lambda d, c: c + d * (nc - 1 - 2 * c)
    return pl.pallas_call(
        functools.partial(_ssd_kernel, nc=nc),
        grid=(2, nc),
        in_specs=[pl.BlockSpec((CHUNK, D_INNER), lambda d, c: (cidx(d, c), 0)),
                  pl.BlockSpec((CHUNK, G_M * N_STATE), lambda d, c: (cidx(d, c), 0)),
                  pl.BlockSpec((CHUNK, G_M * N_STATE), lambda d, c: (cidx(d, c), 0)),
                  pl.BlockSpec((1, CHUNK, D_INNER), lambda d, c: (d, cidx(d, c), 0)),
                  pl.BlockSpec((1, CHUNK, D_INNER), lambda d, c: (d, cidx(d, c), 0)),
                  pl.BlockSpec((1, H_M, CHUNK), lambda d, c: (d, 0, cidx(d, c))),
                  pl.BlockSpec((1, N_STATE, D_INNER), lambda d, c: (d, 0, 0))],
        out_specs=[pl.BlockSpec((1, CHUNK, D_INNER), lambda d, c: (d, cidx(d, c), 0)),
                   pl.BlockSpec((1, N_STATE, D_INNER), lambda d, c: (d, 0, 0))],
        out_shape=[jax.ShapeDtypeStruct((2, s_len, D_INNER), F32),
                   jax.ShapeDtypeStruct((2, N_STATE, D_INNER), F32)],
        scratch_shapes=[pltpu.VMEM((N_STATE, D_INNER), F32)],
        compiler_params=_params(("arbitrary", "arbitrary")),
        name="ssd_scan",
    )(xs, bm, cm, a_exp, dt_exp, a_t, h0)


def _merge_kernel(oa_ref, y_ref, xs_ref, dsk_ref, z_ref,
                  sng_ref, oc_ref, w0_ref, w1_ref, w2_ref, g0_ref, g1_ref, g2_ref,
                  o_ref, br_ref):
    @pl.when(pl.program_id(1) == 0)
    def _():
        z = z_ref[...]
        u = (y_ref[0] + y_ref[1] + xs_ref[...] * dsk_ref[...]) * (z * jax.nn.sigmoid(z))
        gw = D_INNER // G_M
        for g in range(G_M):
            sl = slice(g * gw, (g + 1) * gw)
            ug = u[:, sl]
            ug = ug * lax.rsqrt(jnp.mean(ug * ug, axis=-1, keepdims=True) + EPS)
            br_ref[:, sl] = (ug * sng_ref[:, sl]).astype(BF16)

    acc = jax.nn.sigmoid(g0_ref[...]) * jnp.dot(oa_ref[...], w0_ref[...],
                                                preferred_element_type=F32)
    acc += jax.nn.sigmoid(g1_ref[...]) * jnp.dot(br_ref[...], w1_ref[...],
                                                 preferred_element_type=F32)
    acc += jax.nn.sigmoid(g2_ref[...]) * jnp.dot(oc_ref[...], w2_ref[...],
                                                 preferred_element_type=F32)
    o_ref[...] = acc.astype(o_ref.dtype)


def merge_branches(oa, y, xs, dskip_exp, ssm_norm_g, oc, w0, w1, w2, proj, *, tm=256, tn=512):
    m = oa.shape[0]
    n = D_MODEL
    tm = min(tm, m)
    assert m % tm == 0 and n % tn == 0 and OFF_G % tn == 0 and OFF_Z % D_INNER == 0
    wide = lambda i, j: (i, 0)
    fixed = lambda i, j: (0, 0)
    wspec = pl.BlockSpec((W_A, tn), lambda i, j: (0, j))
    gspec = lambda b: pl.BlockSpec((tm, tn), lambda i, j: (i, (OFF_G + b * n) // tn + j))
    return pl.pallas_call(
        _merge_kernel,
        grid=(m // tm, n // tn),
        in_specs=[pl.BlockSpec((tm, W_A), wide),
                  pl.BlockSpec((2, tm, D_INNER), lambda i, j: (0, i, 0)),
                  pl.BlockSpec((tm, D_INNER), wide),
                  pl.BlockSpec((1, D_INNER), fixed),
                  pl.BlockSpec((tm, D_INNER), lambda i, j: (i, OFF_Z // D_INNER)),
                  pl.BlockSpec((1, D_INNER), fixed),
                  pl.BlockSpec((tm, W_C), wide),
                  wspec, wspec, wspec, gspec(0), gspec(1), gspec(2)],
        out_specs=pl.BlockSpec((tm, tn), lambda i, j: (i, j)),
        out_shape=jax.ShapeDtypeStruct((m, n), BF16),
        scratch_shapes=[pltpu.VMEM((tm, D_INNER), BF16)],
        compiler_params=_params(("parallel", "arbitrary")),
        name="merge_branches",
    )(oa, y, xs, dskip_exp.reshape(1, D_INNER), proj, ssm_norm_g.reshape(1, D_INNER), oc,
      w0, w1, w2, proj, proj, proj)


def _rmsnorm_kernel(x_ref, g_ref, o_ref):
    x = x_ref[...]
    o_ref[...] = x * lax.rsqrt(jnp.mean(x * x, axis=-1, keepdims=True) + EPS) * g_ref[...]


def rmsnorm_rows(x, g, *, tm=512):
    m, k = x.shape
    tm = min(tm, m)
    return pl.pallas_call(
        _rmsnorm_kernel,
        grid=(m // tm,),
        in_specs=[pl.BlockSpec((tm, k), lambda i: (i, 0)), pl.BlockSpec((1, k), lambda i: (0, 0))],
        out_specs=pl.BlockSpec((tm, k), lambda i: (i, 0)),
        out_shape=jax.ShapeDtypeStruct((m, k), F32),
        compiler_params=_params(("parallel",)),
        name="final_rmsnorm",
    )(x, g.reshape(1, k))


def _rope_tables(rows, dim):
    row = jnp.repeat(jnp.arange(rows), GRID_W).astype(F32)
    col = jnp.tile(jnp.arange(GRID_W), rows).astype(F32)
    n_freq = dim // 4
    inv = 1.0 / (ROPE_BASE ** (jnp.arange(n_freq, dtype=F32) / n_freq))
    ang = jnp.concatenate([row[:, None] * inv, col[:, None] * inv], axis=-1)
    return jnp.cos(ang), jnp.sin(ang)


def _rope(t, tables):
    cos, sin = tables
    half = t.shape[-1] // 2
    t1, t2 = t[..., :half], t[..., half:]
    cc, ss = cos[:, None, :], sin[:, None, :]
    return jnp.concatenate([t1 * cc - t2 * ss, t2 * cc + t1 * ss], axis=-1)


def _pad_last(t, width):
    return jnp.pad(t, [(0, 0)] * (t.ndim - 1) + [(0, width - t.shape[-1])])


def _layout_w_in(w):
    offs, acc = [], 0
    for sz in IN_SIZES[:-1]:
        acc += sz
        offs.append(acc)
    parts = jnp.split(w, offs, axis=-1)
    parts[5] = _pad_last(parts[5], LANE)
    parts[8] = _pad_last(parts[8], LANE)
    return jnp.concatenate(parts, axis=-1).astype(BF16)


def _layout_w_uq(w):
    w = w.reshape(Q_LORA, H_C, D_NOPE + D_ROPE)
    return _pad_last(w, 2 * LANE).reshape(Q_LORA, H_C * 2 * LANE).astype(BF16)


def _layout_w_ukv(w):
    w = w.reshape(KV_LORA, H_C, D_NOPE + D_VC)
    return jnp.concatenate([w[..., :D_NOPE].reshape(KV_LORA, H_C * D_NOPE),
                            w[..., D_NOPE:].reshape(KV_LORA, H_C * D_VC)], axis=-1).astype(BF16)


def _dwconv_silu(u, w, bias):
    pad = D_CONV // 2
    up = jnp.pad(u, ((pad, pad), (0, 0)))
    y = sum(up[i:i + u.shape[0]] * w[i] for i in range(D_CONV)) + bias
    return y * jax.nn.sigmoid(y)


def _ssm_operands(proj, conv_w, conv_b, dt_bias, a_neg):
    xbc = _dwconv_silu(proj[:, OFF_XBC:OFF_XBC + CONV_CH], conv_w, conv_b)
    xs = xbc[:, :D_INNER]
    bm = xbc[:, D_INNER:D_INNER + G_M * N_STATE]
    cm = xbc[:, D_INNER + G_M * N_STATE:]
    dt_raw = proj[:, OFF_DT:OFF_DT + 2 * H_M].reshape(-1, 2, H_M)
    dt = jnp.moveaxis(jax.nn.softplus(dt_raw + dt_bias[None]), 1, 0)
    a = dt * a_neg[:, None, :]
    return (xs, bm, cm, jnp.repeat(a, P_M, axis=-1), jnp.repeat(dt, P_M, axis=-1),
            jnp.swapaxes(a, 1, 2))


def _diff_qk(proj, off, tables, scale):
    t = proj[:, off:off + 2 * H_A * DA].reshape(-1, 2, H_A, DA)
    outs = []
    for mp in range(2):
        u = t[:, mp]
        if tables is not None:
            u = _rope(u, tables)
        outs.append((u * scale).reshape(-1, H_A * DA).astype(BF16))
    return outs


def _mla_q(qfull, tables, scale):
    q = qfull.reshape(-1, H_C, 2 * LANE)
    if tables is not None:
        q = jnp.concatenate([q[..., :D_NOPE], _rope(q[..., D_NOPE:D_NOPE + D_ROPE], tables),
                             q[..., D_NOPE + D_ROPE:]], axis=-1)
    return (q * scale).reshape(-1, H_C * 2 * LANE).astype(BF16)


def _mla_k(kvfull, kr, tables):
    s_len = kvfull.shape[0]
    k_nope = kvfull[:, :H_C * D_NOPE].reshape(s_len, H_C, D_NOPE)
    kr = kr[:, None, :]
    if tables is not None:
        kr = _rope(kr, tables)
    kr = jnp.broadcast_to(_pad_last(kr, LANE), (s_len, H_C, LANE))
    return jnp.concatenate([k_nope, kr], axis=-1).reshape(s_len, H_C * 2 * LANE).astype(BF16)


def kernel(x, c, ctx, c_ctx, norm1_g, norm2_g, w_ada, b_ada, w_in, lam_qk, subln_g, w_o_diff,
           conv_w, conv_b, a_log, dt_bias, d_skip, ssm_norm_g, w_o_ssm, q_norm_g, w_uq,
           kv_norm_g, w_ukv, w_o_mla, w_out, w_mlp1, w_mlp2, final_norm_g):
    _, n, _ = x.shape
    rows = n // GRID_W
    rope_a = _rope_tables(rows, DA)
    rope_r = _rope_tables(rows, D_ROPE)
    lat, cx = x[0], ctx[0]
    t_ctx = cx.shape[0]
    sc_a = (DA ** -0.5) * LOG2E
    sc_c = MLA_SCALE * LOG2E
    zero_d = jnp.zeros((D_MODEL,), F32)

    for l in range(DEPTH):
        last = l == DEPTH - 1
        lam_init = 0.8 - 0.6 * math.exp(-0.3 * l)
        a_neg = -jnp.exp(a_log[l].astype(F32))

        cond = jnp.concatenate([c, c_ctx[None], jnp.zeros((6, D_MODEL), F32)], axis=0)
        cond = cond * jax.nn.sigmoid(cond)
        nmod = N_MOD * D_MODEL
        mod = matmul_epilogue(cond, w_ada[l], jnp.zeros((8, nmod), F32), jnp.ones((nmod,), F32),
                              b_ada[l], tm=8, tn=1024, tk=D_MODEL)
        mx = [mod[0, i * D_MODEL:(i + 1) * D_MODEL] for i in range(N_MOD)]
        mc = [mod[1, i * D_MODEL:(i + 1) * D_MODEL] for i in range(N_MOD)]

        w_in_l = _layout_w_in(w_in[l])
        w_uq_l, w_ukv_l = _layout_w_uq(w_uq[l]), _layout_w_ukv(w_ukv[l])
        w_od, w_os, w_om = (w.astype(BF16) for w in (w_o_diff[l], w_o_ssm[l], w_o_mla[l]))
        w_out_l, w1_l, w2_l = (w.astype(BF16) for w in (w_out[l], w_mlp1[l], w_mlp2[l]))
        dskip_exp = jnp.repeat(d_skip[l], P_M)
        zq, zkv = jnp.zeros((Q_LORA,), F32), jnp.zeros((KV_LORA,), F32)

        pc = norm_matmul(cx, norm1_g[l], mc[1], mc[0], w_in_l, tm=256, tn=1280)
        px = norm_matmul(lat, norm1_g[l], mx[1], mx[0], w_in_l, tm=512, tn=1280)

        cq1, cq2 = _diff_qk(pc, OFF_Q, None, sc_a)
        ck1, ck2 = _diff_qk(pc, OFF_K, None, 1.0)
        ckv_up = norm_matmul(pc[:, OFF_CKV:OFF_CKV + KV_LORA], kv_norm_g[l], zkv, zkv, w_ukv_l)
        cmk = _mla_k(ckv_up, pc[:, OFF_KR:OFF_KR + D_ROPE], None)
        c_ssm = _ssm_operands(pc, conv_w[l], conv_b[l], dt_bias[l], a_neg)
        h0 = jnp.zeros((2, N_STATE, D_INNER), F32)
        cy, hfin = ssd_scan(*c_ssm, h0)

        q1, q2 = _diff_qk(px, OFF_Q, rope_a, sc_a)
        k1, k2 = _diff_qk(px, OFF_K, rope_a, 1.0)
        kk1 = jnp.concatenate([ck1, k1], axis=0)
        kk2 = jnp.concatenate([ck2, k2], axis=0)
        vt = jnp.concatenate([pc[:, OFF_V:OFF_V + W_A], px[:, OFF_V:OFF_V + W_A]],
                             axis=0).astype(BF16).T
        oa = diff_flash_attention(lam_qk[l], subln_g[l], q1, q2, kk1, kk2, vt,
                                  lam_init=lam_init, tq=512, tk=1280)

        q_up = norm_matmul(px[:, OFF_CQ:OFF_CQ + Q_LORA], q_norm_g[l], zq, zq, w_uq_l)
        kv_up = norm_matmul(px[:, OFF_CKV:OFF_CKV + KV_LORA], kv_norm_g[l], zkv, zkv, w_ukv_l)
        mq = _mla_q(q_up, rope_r, sc_c)
        mk = jnp.concatenate([cmk, _mla_k(kv_up, px[:, OFF_KR:OFF_KR + D_ROPE], rope_r)], axis=0)
        mvt = jnp.concatenate([ckv_up[:, H_C * D_NOPE:], kv_up[:, H_C * D_NOPE:]],
                              axis=0).astype(BF16).T
        oc = mla_flash_attention(mq, mk, mvt, tq=512, tk=1280)

        x_ssm = _ssm_operands(px, conv_w[l], conv_b[l], dt_bias[l], a_neg)
        y, _ = ssd_scan(*x_ssm, hfin)

        merged = merge_branches(oa, y, x_ssm[0], dskip_exp, ssm_norm_g[l], oc,
                                w_od, w_os, w_om, px)
        s1 = matmul_epilogue(merged, w_out_l, lat, mx[2], zero_d, tm=512, tn=1024, tk=D_MODEL)
        hmid = norm_matmul(s1, norm2_g[l], mx[4], mx[3], w1_l, act="relu2", out_dtype=BF16,
                           tm=512, tn=1024)
        new_lat = matmul_epilogue(hmid, w2_l, s1, mx[5], zero_d, tm=512, tn=1024, tk=2048)

        if not last:
            coa = diff_flash_attention(lam_qk[l], subln_g[l], cq1, cq2, ck1, ck2, vt[:, :t_ctx],
                                       lam_init=lam_init, tq=256, tk=256)
            cq_up = norm_matmul(pc[:, OFF_CQ:OFF_CQ + Q_LORA], q_norm_g[l], zq, zq, w_uq_l)
            coc = mla_flash_attention(_mla_q(cq_up, None, sc_c), cmk, mvt[:, :t_ctx],
                                      tq=256, tk=256)
            cmerged = merge_branches(coa, cy, c_ssm[0], dskip_exp, ssm_norm_g[l], coc,
                                     w_od, w_os, w_om, pc)
            cs1 = matmul_epilogue(cmerged, w_out_l, cx, mc[2], zero_d, tm=256, tn=1024, tk=D_MODEL)
            chid = norm_matmul(cs1, norm2_g[l], mc[4], mc[3], w1_l, act="relu2", out_dtype=BF16,
                               tm=256, tn=1024)
            cx = matmul_epilogue(chid, w2_l, cs1, mc[5], zero_d, tm=256, tn=1024, tk=2048)
        lat = new_lat

    return rmsnorm_rows(lat, final_norm_g)[None]
```

```python
import functools
import math

import jax
import jax.numpy as jnp
from jax import lax
from jax.experimental import pallas as pl
from jax.experimental.pallas import tpu as pltpu

D_MODEL = 2048
DEPTH = 2
GRID_W = 64
ROPE_BASE = 10000.0
EPS = 1e-6

H_A = 8
DA = 64
DV_A = 2 * DA
W_A = H_A * DV_A

H_M = 16
P_M = 64
D_INNER = H_M * P_M
N_STATE = 128
G_M = 2
D_CONV = 5
CHUNK = 128
CONV_CH = D_INNER + 2 * G_M * N_STATE

H_C = 8
D_NOPE = 128
D_ROPE = 64
D_VC = 128
Q_LORA = 512
KV_LORA = 256
W_C = H_C * D_VC
MLA_SCALE = (D_NOPE + D_ROPE) ** -0.5

D_FF = 4 * D_MODEL
N_BRANCH = 3
N_MOD = 6

IN_SIZES = (2 * H_A * DA, 2 * H_A * DA, W_A, D_INNER, CONV_CH, 2 * H_M,
            Q_LORA, KV_LORA, D_ROPE, N_BRANCH * D_MODEL)

LANE = 128
ONES_ROWS = 16
LOG2E = 1.4426950408889634
VMEM_LIMIT = 48 * 1024 * 1024

BF16 = jnp.bfloat16
F32 = jnp.float32

OFF_Q, OFF_K, OFF_V, OFF_Z, OFF_XBC = 0, 1024, 2048, 3072, 4096
OFF_DT = OFF_XBC + CONV_CH
OFF_CQ = OFF_DT + LANE
OFF_CKV = OFF_CQ + Q_LORA
OFF_KR = OFF_CKV + KV_LORA
OFF_G = OFF_KR + LANE
P_PAD = OFF_G + N_BRANCH * D_MODEL


def _params(sem):
    return pltpu.CompilerParams(dimension_semantics=sem, vmem_limit_bytes=VMEM_LIMIT)


def _norm_mm_kernel(x_ref, g_ref, sc_ref, sh_ref, w_ref, o_ref, h_ref, *, act):
    @pl.when(pl.program_id(1) == 0)
    def _():
        x = x_ref[...]
        y = x * lax.rsqrt(jnp.mean(x * x, axis=-1, keepdims=True) + EPS)
        h = y * g_ref[...] * (1.0 + sc_ref[...]) + sh_ref[...]
        h_ref[...] = h.astype(BF16)

    acc = jnp.dot(h_ref[...], w_ref[...], preferred_element_type=F32)
    if act == "relu2":
        acc = jnp.square(jnp.maximum(acc, 0.0))
    o_ref[...] = acc.astype(o_ref.dtype)


def norm_matmul(x, g, scale, shift, w, *, act=None, out_dtype=F32, tm=512, tn=512):
    m, k = x.shape
    n = w.shape[1]
    tm, tn = min(tm, m), min(tn, n)
    assert m % tm == 0 and n % tn == 0
    row = pl.BlockSpec((1, k), lambda i, j: (0, 0))
    return pl.pallas_call(
        functools.partial(_norm_mm_kernel, act=act),
        grid=(m // tm, n // tn),
        in_specs=[pl.BlockSpec((tm, k), lambda i, j: (i, 0)), row, row, row,
                  pl.BlockSpec((k, tn), lambda i, j: (0, j))],
        out_specs=pl.BlockSpec((tm, tn), lambda i, j: (i, j)),
        out_shape=jax.ShapeDtypeStruct((m, n), out_dtype),
        scratch_shapes=[pltpu.VMEM((tm, k), BF16)],
        compiler_params=_params(("parallel", "arbitrary")),
        name="norm_matmul",
    )(x, g.reshape(1, k), scale.reshape(1, k), shift.reshape(1, k), w)


def _mm_kernel(a_ref, w_ref, res_ref, gate_ref, bias_ref, o_ref, acc_ref, *, nk):
    kk = pl.program_id(2)

    @pl.when(kk == 0)
    def _():
        acc_ref[...] = jnp.zeros_like(acc_ref)

    acc_ref[...] += jnp.dot(a_ref[...].astype(BF16), w_ref[...].astype(BF16),
                            preferred_element_type=F32)

    @pl.when(kk == nk - 1)
    def _():
        o_ref[...] = (res_ref[...] + gate_ref[...] * acc_ref[...] + bias_ref[...]
                      ).astype(o_ref.dtype)


def matmul_epilogue(a, w, res, gate, bias, *, out_dtype=F32, tm=512, tn=1024, tk=2048):
    m, k = a.shape
    n = w.shape[1]
    tm, tn, tk = min(tm, m), min(tn, n), min(tk, k)
    assert m % tm == 0 and n % tn == 0 and k % tk == 0
    nk = k // tk
    return pl.pallas_call(
        functools.partial(_mm_kernel, nk=nk),
        grid=(m // tm, n // tn, nk),
        in_specs=[pl.BlockSpec((tm, tk), lambda i, j, kk: (i, kk)),
                  pl.BlockSpec((tk, tn), lambda i, j, kk: (kk, j)),
                  pl.BlockSpec((tm, tn), lambda i, j, kk: (i, j)),
                  pl.BlockSpec((1, tn), lambda i, j, kk: (0, j)),
                  pl.BlockSpec((1, tn), lambda i, j, kk: (0, j))],
        out_specs=pl.BlockSpec((tm, tn), lambda i, j, kk: (i, j)),
        out_shape=jax.ShapeDtypeStruct((m, n), out_dtype),
        scratch_shapes=[pltpu.VMEM((tm, tn), F32)],
        compiler_params=_params(("parallel", "parallel", "arbitrary")),
        name="matmul_epilogue",
    )(a, w, res, gate.reshape(1, n), bias.reshape(1, n))


def _flash_init(m_ref, acc_ref):
    m_ref[...] = jnp.full_like(m_ref, -jnp.inf)
    acc_ref[...] = jnp.zeros_like(acc_ref)


def _flash_streams(streams, s_ref, p_ref, m_ref, acc_ref):
    n = len(streams)
    tk = s_ref.shape[1]
    ones = jnp.ones((ONES_ROWS, tk), BF16)
    alpha = [None] * n
    vts = [None] * n
    for t in range(n + 2):
        if t < n:
            kx, qx, vts[t] = streams[t]()
            s_ref[t % 2] = lax.dot_general(kx, qx, (((1,), (1,)), ((), ())),
                                           preferred_element_type=F32)
        u = t - 1
        if 0 <= u < n:
            m_prev = m_ref[u]
            m_new = jnp.maximum(m_prev, jnp.max(s_ref[u % 2], axis=0, keepdims=True))
            m_ref[u] = m_new
            alpha[u] = jnp.exp2(m_prev - m_new)
            p_ref[u % 2] = jnp.exp2(s_ref[u % 2] - m_new).astype(BF16)
        w = t - 2
        if 0 <= w < n:
            vt_ext = jnp.concatenate([vts[w], ones], axis=0)
            acc_ref[w] = alpha[w] * acc_ref[w] + jnp.dot(vt_ext, p_ref[w % 2],
                                                         preferred_element_type=F32)


def _diff_flash_kernel(lamqk_ref, subg_ref, q1_ref, q2_ref, k1_ref, k2_ref, vt_ref, o_ref,
                       s_ref, p_ref, m_ref, acc_ref, *, nk, lam_init):
    j = pl.program_id(2)

    @pl.when(j == 0)
    def _():
        _flash_init(m_ref, acc_ref)

    lane = lax.broadcasted_iota(jnp.int32, q1_ref.shape, 1)

    def stream(q_ref, k_ref, half):
        def load():
            q = q_ref[...]
            qh = jnp.where(lane >= DA if half else lane < DA, q, jnp.zeros_like(q))
            return k_ref[...], qh, vt_ref[half * DV_A:(half + 1) * DV_A, :]
        return load

    _flash_streams([stream(q1_ref, k1_ref, 0), stream(q1_ref, k1_ref, 1),
                    stream(q2_ref, k2_ref, 0), stream(q2_ref, k2_ref, 1)],
                   s_ref, p_ref, m_ref, acc_ref)

    @pl.when(j == nk - 1)
    def _():
        lq = lamqk_ref[...]
        lam = (jnp.exp(jnp.sum(lq[0:1] * lq[1:2], axis=-1, keepdims=True))
               - jnp.exp(jnp.sum(lq[2:3] * lq[3:4], axis=-1, keepdims=True)) + lam_init)
        norm = lambda i: acc_ref[i, :DV_A, :] / acc_ref[i, DV_A:DV_A + 1, :]
        for half in range(2):
            o = norm(half) - lam * norm(2 + half)
            o = o * lax.rsqrt(jnp.mean(o * o, axis=0, keepdims=True) + EPS)
            o = o * subg_ref[...] * (1.0 - lam_init)
            o_ref[:, half * DV_A:(half + 1) * DV_A] = o.T.astype(o_ref.dtype)


def diff_flash_attention(lam_qk, subln_g, q1, q2, k1, k2, vt, *, lam_init, tq, tk):
    s_len, t_len = q1.shape[0], k1.shape[0]
    tq, tk = min(tq, s_len), min(tk, t_len)
    assert s_len % tq == 0 and t_len % tk == 0
    nk = t_len // tk
    qspec = pl.BlockSpec((tq, LANE), lambda h, i, j: (i, h))
    kspec = pl.BlockSpec((tk, LANE), lambda h, i, j: (j, h))
    fixed = lambda h, i, j: (0, 0)
    return pl.pallas_call(
        functools.partial(_diff_flash_kernel, nk=nk, lam_init=lam_init),
        grid=(H_A // 2, s_len // tq, nk),
        in_specs=[pl.BlockSpec((4, DA), fixed), pl.BlockSpec((DV_A, 1), fixed),
                  qspec, qspec, kspec, kspec,
                  pl.BlockSpec((2 * DV_A, tk), lambda h, i, j: (h, j))],
        out_specs=pl.BlockSpec((tq, 2 * DV_A), lambda h, i, j: (i, h)),
        out_shape=jax.ShapeDtypeStruct((s_len, W_A), BF16),
        scratch_shapes=[pltpu.VMEM((2, tk, tq), F32), pltpu.VMEM((2, tk, tq), BF16),
                        pltpu.VMEM((4, 1, tq), F32),
                        pltpu.VMEM((4, DV_A + ONES_ROWS, tq), F32)],
        compiler_params=_params(("parallel", "parallel", "arbitrary")),
        name="diff_flash",
    )(lam_qk, subln_g.reshape(DV_A, 1), q1, q2, k1, k2, vt)


def _mla_flash_kernel(q_ref, k_ref, vt_ref, o_ref, s_ref, p_ref, m_ref, acc_ref, *, nk, hps):
    j = pl.program_id(2)

    @pl.when(j == 0)
    def _():
        _flash_init(m_ref, acc_ref)

    dk = 2 * LANE

    def stream(hh):
        return lambda: (k_ref[:, hh * dk:(hh + 1) * dk], q_ref[:, hh * dk:(hh + 1) * dk],
                        vt_ref[hh * D_VC:(hh + 1) * D_VC, :])

    _flash_streams([stream(hh) for hh in range(hps)], s_ref, p_ref, m_ref, acc_ref)

    @pl.when(j == nk - 1)
    def _():
        for hh in range(hps):
            o = acc_ref[hh, :D_VC, :] / acc_ref[hh, D_VC:D_VC + 1, :]
            o_ref[:, hh * D_VC:(hh + 1) * D_VC] = o.T.astype(o_ref.dtype)


def mla_flash_attention(q, k, vt, *, tq, tk, hps=4):
    s_len, t_len = q.shape[0], k.shape[0]
    tq, tk = min(tq, s_len), min(tk, t_len)
    assert s_len % tq == 0 and t_len % tk == 0 and H_C % hps == 0
    nk = t_len // tk
    dk = 2 * LANE
    return pl.pallas_call(
        functools.partial(_mla_flash_kernel, nk=nk, hps=hps),
        grid=(H_C // hps, s_len // tq, nk),
        in_specs=[pl.BlockSpec((tq, hps * dk), lambda h, i, j: (i, h)),
                  pl.BlockSpec((tk, hps * dk), lambda h, i, j: (j, h)),
                  pl.BlockSpec((hps * D_VC, tk), lambda h, i, j: (h, j))],
        out_specs=pl.BlockSpec((tq, hps * D_VC), lambda h, i, j: (i, h)),
        out_shape=jax.ShapeDtypeStruct((s_len, W_C), BF16),
        scratch_shapes=[pltpu.VMEM((2, tk, tq), F32), pltpu.VMEM((2, tk, tq), BF16),
                        pltpu.VMEM((hps, 1, tq), F32),
                        pltpu.VMEM((hps, D_VC + ONES_ROWS, tq), F32)],
        compiler_params=_params(("parallel", "parallel", "arbitrary")),
        name="mla_flash",
    )(q, k, vt)


def _split3(a):
    a1 = a.astype(BF16)
    r1 = a - a1.astype(F32)
    a2 = r1.astype(BF16)
    a3 = (r1 - a2.astype(F32)).astype(BF16)
    return a1, a2, a3


def _ssd_kernel(xs_ref, b_ref, c_ref, aexp_ref, dtexp_ref, at_ref, h0_ref,
                y_ref, hfin_ref, h_scr, *, nc):
    d = pl.program_id(0)
    c = pl.program_id(1)

    @pl.when(c == 0)
    def _():
        h_scr[...] = h0_ref[0]

    row = lax.broadcasted_iota(jnp.int32, (CHUNK, CHUNK), 0)
    col = lax.broadcasted_iota(jnp.int32, (CHUNK, CHUNK), 1)
    mask = jnp.where(d == 0, row - col, col - row) >= 0
    tc = mask.astype(BF16)

    a_exp = aexp_ref[0]
    cum = sum(jnp.dot(tc, piece, preferred_element_type=F32) for piece in _split3(a_exp))
    total = jnp.sum(a_exp, axis=0, keepdims=True)
    cum_t = sum(lax.dot_general(piece, tc, (((1,), (1,)), ((), ())),
                                preferred_element_type=F32)
                for piece in _split3(at_ref[0]))

    xd = xs_ref[...] * dtexp_ref[0]
    xdd = (xd * jnp.exp(total - cum)).astype(BF16)
    xd_b = xd.astype(BF16)
    h_prev = h_scr[...]
    h_prev_b = h_prev.astype(BF16)
    lane = lax.broadcasted_iota(jnp.int32, (CHUNK, LANE), 1)
    gw = D_INNER // G_M

    y_parts, st_parts = [], []
    for g in range(G_M):
        bg = b_ref[:, g * N_STATE:(g + 1) * N_STATE]
        cg = c_ref[:, g * N_STATE:(g + 1) * N_STATE].astype(BF16)
        cb = lax.dot_general(cg, bg.astype(BF16), (((1,), (1,)), ((), ())),
                             preferred_element_type=F32)
        y_off = jnp.dot(cg, h_prev_b[:, g * gw:(g + 1) * gw], preferred_element_type=F32)
        st_parts.append(jnp.dot(bg.T.astype(BF16), xdd[:, g * gw:(g + 1) * gw],
                                preferred_element_type=F32))
        for pr in range(gw // LANE):
            lo = g * gw + pr * LANE
            xp = xd_b[:, lo:lo + LANE]
            halves = (jnp.where(lane < P_M, xp, 0), jnp.where(lane >= P_M, xp, 0))
            y_pair = y_off[:, pr * LANE:(pr + 1) * LANE] * jnp.exp(cum[:, lo:lo + LANE])
            for half in range(2):
                hh = lo // P_M + half
                seg = cum[:, hh * P_M:hh * P_M + 1] - cum_t[hh:hh + 1, :]
                mm = (jnp.exp(jnp.where(mask, seg, -jnp.inf)) * cb).astype(BF16)
                y_pair += jnp.dot(mm, halves[half], preferred_element_type=F32)
            y_parts.append(y_pair)

    y_ref[0] = jnp.concatenate(y_parts, axis=1)
    h_new = h_prev * jnp.exp(total) + jnp.concatenate(st_parts, axis=1)
    h_scr[...] = h_new

    @pl.when(c == nc - 1)
    def _():
        hfin_ref[0] = h_new


def ssd_scan(xs, bm, cm, a_exp, dt_exp, a_t, h0):
    s_len = xs.shape[0]
    nc = s_len // CHUNK
    cidx = lambda d, c: c + d * (nc - 1 - 2 * c)
    return pl.pallas_call(
        functools.partial(_ssd_kernel, nc=nc),
        grid=(2, nc),
        in_specs=[pl.BlockSpec((CHUNK, D_INNER), lambda d, c: (cidx(d, c), 0)),
                  pl.BlockSpec((CHUNK, G_M * N_STATE), lambda d, c: (cidx(d, c), 0)),
                  pl.BlockSpec((CHUNK, G_M * N_STATE), lambda d, c: (cidx(d, c), 0)),
                  pl.BlockSpec((1, CHUNK, D_INNER), lambda d, c: (d, cidx(d, c), 0)),
                  pl.BlockSpec((1, CHUNK, D_INNER), lambda d, c: (d, cidx(d, c), 0)),
                  pl.BlockSpec((1, H_M, CHUNK), lambda d, c: (d, 0, cidx(d, c))),
                  pl.BlockSpec((1, N_STATE, D_INNER), lambda d, c: (d, 0, 0))],
        out_specs=[pl.BlockSpec((1, CHUNK, D_INNER), lambda d, c: (d, cidx(d, c), 0)),
                   pl.BlockSpec((1, N_STATE, D_INNER), lambda d, c: (d, 0, 0))],
        out_shape=[jax.ShapeDtypeStruct((2, s_len, D_INNER), F32),
                   jax.ShapeDtypeStruct((2, N_STATE, D_INNER), F32)],
        scratch_shapes=[pltpu.VMEM((N_STATE, D_INNER), F32)],
        compiler_params=_params(("arbitrary", "arbitrary")),
        name="ssd_scan",
    )(xs, bm, cm, a_exp, dt_exp, a_t, h0)


def _merge_kernel(oa_ref, y_ref, xs_ref, dsk_ref, z_ref,
                  sng_ref, oc_ref, w0_ref, w1_ref, w2_ref, g0_ref, g1_ref, g2_ref,
                  o_ref, br_ref):
    @pl.when(pl.program_id(1) == 0)
    def _():
        z = z_ref[...]
        u = (y_ref[0] + y_ref[1] + xs_ref[...] * dsk_ref[...]) * (z * jax.nn.sigmoid(z))
        gw = D_INNER // G_M
        for g in range(G_M):
            sl = slice(g * gw, (g + 1) * gw)
            ug = u[:, sl]
            ug = ug * lax.rsqrt(jnp.mean(ug * ug, axis=-1, keepdims=True) + EPS)
            br_ref[:, sl] = (ug * sng_ref[:, sl]).astype(BF16)

    acc = jax.nn.sigmoid(g0_ref[...]) * jnp.dot(oa_ref[...], w0_ref[...],
                                                preferred_element_type=F32)
    acc += jax.nn.sigmoid(g1_ref[...]) * jnp.dot(br_ref[...], w1_ref[...],
                                                 preferred_element_type=F32)
    acc += jax.nn.sigmoid(g2_ref[...]) * jnp.dot(oc_ref[...], w2_ref[...],
                                                 preferred_element_type=F32)
    o_ref[...] = acc.astype(o_ref.dtype)


def merge_branches(oa, y, xs, dskip_exp, ssm_norm_g, oc, w0, w1, w2, proj, *, tm=256, tn=512):
    m = oa.shape[0]
    n = D_MODEL
    tm = min(tm, m)
    assert m % tm == 0 and n % tn == 0 and OFF_G % tn == 0 and OFF_Z % D_INNER == 0
    wide = lambda i, j: (i, 0)
    fixed = lambda i, j: (0, 0)
    wspec = pl.BlockSpec((W_A, tn), lambda i, j: (0, j))
    gspec = lambda b: pl.BlockSpec((tm, tn), lambda i, j: (i, (OFF_G + b * n) // tn + j))
    return pl.pallas_call(
        _merge_kernel,
        grid=(m // tm, n // tn),
        in_specs=[pl.BlockSpec((tm, W_A), wide),
                  pl.BlockSpec((2, tm, D_INNER), lambda i, j: (0, i, 0)),
                  pl.BlockSpec((tm, D_INNER), wide),
                  pl.BlockSpec((1, D_INNER), fixed),
                  pl.BlockSpec((tm, D_INNER), lambda i, j: (i, OFF_Z // D_INNER)),
                  pl.BlockSpec((1, D_INNER), fixed),
                  pl.BlockSpec((tm, W_C), wide),
                  wspec, wspec, wspec, gspec(0), gspec(1), gspec(2)],
        out_specs=pl.BlockSpec((tm, tn), lambda i, j: (i, j)),
        out_shape=jax.ShapeDtypeStruct((m, n), BF16),
        scratch_shapes=[pltpu.VMEM((tm, D_INNER), BF16)],
        compiler_params=_params(("parallel", "arbitrary")),
        name="merge_branches",
    )(oa, y, xs, dskip_exp.reshape(1, D_INNER), proj, ssm_norm_g.reshape(1, D_INNER), oc,
      w0, w1, w2, proj, proj, proj)


def _rmsnorm_kernel(x_ref, g_ref, o_ref):
    x = x_ref[...]
    o_ref[...] = x * lax.rsqrt(jnp.mean(x * x, axis=-1, keepdims=True) + EPS) * g_ref[...]


def rmsnorm_rows(x, g, *, tm=512):
    m, k = x.shape
    tm = min(tm, m)
    return pl.pallas_call(
        _rmsnorm_kernel,
        grid=(m // tm,),
        in_specs=[pl.BlockSpec((tm, k), lambda i: (i, 0)), pl.BlockSpec((1, k), lambda i: (0, 0))],
        out_specs=pl.BlockSpec((tm, k), lambda i: (i, 0)),
        out_shape=jax.ShapeDtypeStruct((m, k), F32),
        compiler_params=_params(("parallel",)),
        name="final_rmsnorm",
    )(x, g.reshape(1, k))


def _rope_tables(rows, dim):
    row = jnp.repeat(jnp.arange(rows), GRID_W).astype(F32)
    col = jnp.tile(jnp.arange(GRID_W), rows).astype(F32)
    n_freq = dim // 4
    inv = 1.0 / (ROPE_BASE ** (jnp.arange(n_freq, dtype=F32) / n_freq))
    ang = jnp.concatenate([row[:, None] * inv, col[:, None] * inv], axis=-1)
    return jnp.cos(ang), jnp.sin(ang)


def _rope(t, tables):
    cos, sin = tables
    half = t.shape[-1] // 2
    t1, t2 = t[..., :half], t[..., half:]
    cc, ss = cos[:, None, :], sin[:, None, :]
    return jnp.concatenate([t1 * cc - t2 * ss, t2 * cc + t1 * ss], axis=-1)


def _pad_last(t, width):
    return jnp.pad(t, [(0, 0)] * (t.ndim - 1) + [(0, width - t.shape[-1])])


def _layout_w_in(w):
    offs, acc = [], 0
    for sz in IN_SIZES[:-1]:
        acc += sz
        offs.append(acc)
    parts = jnp.split(w, offs, axis=-1)
    parts[5] = _pad_last(parts[5], LANE)
    parts[8] = _pad_last(parts[8], LANE)
    return jnp.concatenate(parts, axis=-1).astype(BF16)


def _layout_w_uq(w):
    w = w.reshape(Q_LORA, H_C, D_NOPE + D_ROPE)
    return _pad_last(w, 2 * LANE).reshape(Q_LORA, H_C * 2 * LANE).astype(BF16)


def _layout_w_ukv(w):
    w = w.reshape(KV_LORA, H_C, D_NOPE + D_VC)
    return jnp.concatenate([w[..., :D_NOPE].reshape(KV_LORA, H_C * D_NOPE),
                            w[..., D_NOPE:].reshape(KV_LORA, H_C * D_VC)], axis=-1).astype(BF16)


def _dwconv_silu(u, w, bias):
    pad = D_CONV // 2
    up = jnp.pad(u, ((pad, pad), (0, 0)))
    y = sum(up[i:i + u.shape[0]] * w[i] for i in range(D_CONV)) + bias
    return y * jax.nn.sigmoid(y)


def _ssm_operands(proj, conv_w, conv_b, dt_bias, a_neg):
    xbc = _dwconv_silu(proj[:, OFF_XBC:OFF_XBC + CONV_CH], conv_w, conv_b)
    xs = xbc[:, :D_INNER]
    bm = xbc[:, D_INNER:D_INNER + G_M * N_STATE]
    cm = xbc[:, D_INNER + G_M * N_STATE:]
    dt_raw = proj[:, OFF_DT:OFF_DT + 2 * H_M].reshape(-1, 2, H_M)
    dt = jnp.moveaxis(jax.nn.softplus(dt_raw + dt_bias[None]), 1, 0)
    a = dt * a_neg[:, None, :]
    return (xs, bm, cm, jnp.repeat(a, P_M, axis=-1), jnp.repeat(dt, P_M, axis=-1),
            jnp.swapaxes(a, 1, 2))


def _diff_qk(proj, off, tables, scale):
    t = proj[:, off:off + 2 * H_A * DA].reshape(-1, 2, H_A, DA)
    outs = []
    for mp in range(2):
        u = t[:, mp]
        if tables is not None:
            u = _rope(u, tables)
        outs.append((u * scale).reshape(-1, H_A * DA).astype(BF16))
    return outs


def _mla_q(qfull, tables, scale):
    q = qfull.reshape(-1, H_C, 2 * LANE)
    if tables is not None:
        q = jnp.concatenate([q[..., :D_NOPE], _rope(q[..., D_NOPE:D_NOPE + D_ROPE], tables),
                             q[..., D_NOPE + D_ROPE:]], axis=-1)
    return (q * scale).reshape(-1, H_C * 2 * LANE).astype(BF16)


def _mla_k(kvfull, kr, tables):
    s_len = kvfull.shape[0]
    k_nope = kvfull[:, :H_C * D_NOPE].reshape(s_len, H_C, D_NOPE)
    kr = kr[:, None, :]
    if tables is not None:
        kr = _rope(kr, tables)
    kr = jnp.broadcast_to(_pad_last(kr, LANE), (s_len, H_C, LANE))
    return jnp.concatenate([k_nope, kr], axis=-1).reshape(s_len, H_C * 2 * LANE).astype(BF16)


def kernel(x, c, ctx, c_ctx, norm1_g, norm2_g, w_ada, b_ada, w_in, lam_qk, subln_g, w_o_diff,
           conv_w, conv_b, a_log, dt_bias, d_skip, ssm_norm_g, w_o_ssm, q_norm_g, w_uq,
           kv_norm_g, w_ukv, w_o_mla, w_out, w_mlp1, w_mlp2, final_norm_g):
    _, n, _ = x.shape
    rows = n // GRID_W
    rope_a = _rope_tables(rows, DA)
    rope_r = _rope_tables(rows, D_ROPE)
    lat, cx = x[0], ctx[0]
    t_ctx = cx.shape[0]
    sc_a = (DA ** -0.5) * LOG2E
    sc_c = MLA_SCALE * LOG2E
    zero_d = jnp.zeros((D_MODEL,), F32)

    for l in range(DEPTH):
        last = l == DEPTH - 1
        lam_init = 0.8 - 0.6 * math.exp(-0.3 * l)
        a_neg = -jnp.exp(a_log[l].astype(F32))

        cond = jnp.concatenate([c, c_ctx[None], jnp.zeros((6, D_MODEL), F32)], axis=0)
        cond = cond * jax.nn.sigmoid(cond)
        nmod = N_MOD * D_MODEL
        mod = matmul_epilogue(cond, w_ada[l], jnp.zeros((8, nmod), F32), jnp.ones((nmod,), F32),
                              b_ada[l], tm=8, tn=1024, tk=D_MODEL)
        mx = [mod[0, i * D_MODEL:(i + 1) * D_MODEL] for i in range(N_MOD)]
        mc = [mod[1, i * D_MODEL:(i + 1) * D_MODEL] for i in range(N_MOD)]

        w_in_l = _layout_w_in(w_in[l])
        w_uq_l, w_ukv_l = _layout_w_uq(w_uq[l]), _layout_w_ukv(w_ukv[l])
        w_od, w_os, w_om = (w.astype(BF16) for w in (w_o_diff[l], w_o_ssm[l], w_o_mla[l]))
        w_out_l, w1_l, w2_l = (w.astype(BF16) for w in (w_out[l], w_mlp1[l], w_mlp2[l]))
        dskip_exp = jnp.repeat(d_skip[l], P_M)
        zq, zkv = jnp.zeros((Q_LORA,), F32), jnp.zeros((KV_LORA,), F32)

        pc = norm_matmul(cx, norm1_g[l], mc[1], mc[0], w_in_l, tm=256, tn=1280)
        px = norm_matmul(lat, norm1_g[l], mx[1], mx[0], w_in_l, tm=512, tn=1280)

        cq1, cq2 = _diff_qk(pc, OFF_Q, None, sc_a)
        ck1, ck2 = _diff_qk(pc, OFF_K, None, 1.0)
        ckv_up = norm_matmul(pc[:, OFF_CKV:OFF_CKV + KV_LORA], kv_norm_g[l], zkv, zkv, w_ukv_l)
        cmk = _mla_k(ckv_up, pc[:, OFF_KR:OFF_KR + D_ROPE], None)
        c_ssm = _ssm_operands(pc, conv_w[l], conv_b[l], dt_bias[l], a_neg)
        h0 = jnp.zeros((2, N_STATE, D_INNER), F32)
        cy, hfin = ssd_scan(*c_ssm, h0)

        q1, q2 = _diff_qk(px, OFF_Q, rope_a, sc_a)
        k1, k2 = _diff_qk(px, OFF_K, rope_a, 1.0)
        kk1 = jnp.concatenate([ck1, k1], axis=0)
        kk2 = jnp.concatenate([ck2, k2], axis=0)
        vt = jnp.concatenate([pc[:, OFF_V:OFF_V + W_A], px[:, OFF_V:OFF_V + W_A]],
                             axis=0).astype(BF16).T
        oa = diff_flash_attention(lam_qk[l], subln_g[l], q1, q2, kk1, kk2, vt,
                                  lam_init=lam_init, tq=512, tk=3328)

        q_up = norm_matmul(px[:, OFF_CQ:OFF_CQ + Q_LORA], q_norm_g[l], zq, zq, w_uq_l)
        kv_up = norm_matmul(px[:, OFF_CKV:OFF_CKV + KV_LORA], kv_norm_g[l], zkv, zkv, w_ukv_l)
        mq = _mla_q(q_up, rope_r, sc_c)
        mk = jnp.concatenate([cmk, _mla_k(kv_up, px[:, OFF_KR:OFF_KR + D_ROPE], rope_r)], axis=0)
        mvt = jnp.concatenate([ckv_up[:, H_C * D_NOPE:], kv_up[:, H_C * D_NOPE:]],
                              axis=0).astype(BF16).T
        oc = mla_flash_attention(mq, mk, mvt, tq=512, tk=3328)

        x_ssm = _ssm_operands(px, conv_w[l], conv_b[l], dt_bias[l], a_neg)
        y, _ = ssd_scan(*x_ssm, hfin)

        merged = merge_branches(oa, y, x_ssm[0], dskip_exp, ssm_norm_g[l], oc,
                                w_od, w_os, w_om, px)
        s1 = matmul_epilogue(merged, w_out_l, lat, mx[2], zero_d, tm=512, tn=1024, tk=D_MODEL)
        hmid = norm_matmul(s1, norm2_g[l], mx[4], mx[3], w1_l, act="relu2", out_dtype=BF16,
                           tm=512, tn=1024)
        new_lat = matmul_epilogue(hmid, w2_l, s1, mx[5], zero_d, tm=512, tn=1024, tk=2048)

        if not last:
            coa = diff_flash_attention(lam_qk[l], subln_g[l], cq1, cq2, ck1, ck2, vt[:, :t_ctx],
                                       lam_init=lam_init, tq=256, tk=256)
            cq_up = norm_matmul(pc[:, OFF_CQ:OFF_CQ + Q_LORA], q_norm_g[l], zq, zq, w_uq_l)
            coc = mla_flash_attention(_mla_q(cq_up, None, sc_c), cmk, mvt[:, :t_ctx],
                                      tq=256, tk=256)
            cmerged = merge_branches(coa, cy, c_ssm[0], dskip_exp, ssm_norm_g[l], coc,
                                     w_od, w_os, w_om, pc)
            cs1 = matmul_epilogue(cmerged, w_out_l, cx, mc[2], zero_d, tm=256, tn=1024, tk=D_MODEL)
            chid = norm_matmul(cs1, norm2_g[l], mc[4], mc[3], w1_l, act="relu2", out_dtype=BF16,
                               tm=256, tn=1024)
            cx = matmul_epilogue(chid, w2_l, cs1, mc[5], zero_d, tm=256, tn=1024, tk=2048)
        lat = new_lat

    return rmsnorm_rows(lat, final_norm_g)[None]
```

```python
import functools
import math

import jax
import jax.numpy as jnp
from jax import lax
from jax.experimental import pallas as pl
from jax.experimental.pallas import tpu as pltpu

D_MODEL = 2048
DEPTH = 2
GRID_W = 64
ROPE_BASE = 10000.0
EPS = 1e-6

H_A = 8
DA = 64
DV_A = 2 * DA
W_A = H_A * DV_A

H_M = 16
P_M = 64
D_INNER = H_M * P_M
N_STATE = 128
G_M = 2
D_CONV = 5
CHUNK = 128
CONV_CH = D_INNER + 2 * G_M * N_STATE

H_C = 8
D_NOPE = 128
D_ROPE = 64
D_VC = 128
Q_LORA = 512
KV_LORA = 256
W_C = H_C * D_VC
MLA_SCALE = (D_NOPE + D_ROPE) ** -0.5

D_FF = 4 * D_MODEL
N_BRANCH = 3
N_MOD = 6

IN_SIZES = (2 * H_A * DA, 2 * H_A * DA, W_A, D_INNER, CONV_CH, 2 * H_M,
            Q_LORA, KV_LORA, D_ROPE, N_BRANCH * D_MODEL)

LANE = 128
ONES_ROWS = 16
KV_BLOCK_MAX = 3328
LOG2E = 1.4426950408889634
VMEM_LIMIT = 48 * 1024 * 1024

BF16 = jnp.bfloat16
F32 = jnp.float32

assert DA == D_ROPE and LANE % DA == 0
SSM_Z, SSM_XBC, SSM_DT = 0, D_INNER, D_INNER + CONV_CH
SSM_W = SSM_DT + LANE


def _params(sem):
    return pltpu.CompilerParams(dimension_semantics=sem, vmem_limit_bytes=VMEM_LIMIT)


def _rope_lanes(a, c, sa, sb):
    return a * c + pltpu.roll(a, LANE - DA // 2, 1) * sa + pltpu.roll(a, DA // 2, 1) * sb


def _proj_kernel(*refs, n_lat, epilogue, rope_every):
    if epilogue == "rope":
        x_ref, g_ref, sc_ref, sh_ref, w_ref, cs_ref, c_ref, sa_ref, sb_ref, o_ref, h_ref = refs
    else:
        x_ref, g_ref, sc_ref, sh_ref, w_ref, o_ref, h_ref = refs
    tm = x_ref.shape[0]
    row0 = pl.program_id(0) * tm

    @pl.when(pl.program_id(1) == 0)
    def _():
        x = x_ref[...]
        y = x * lax.rsqrt(jnp.mean(x * x, axis=-1, keepdims=True) + EPS)
        row = row0 + lax.broadcasted_iota(jnp.int32, (tm, 1), 0)
        is_ctx = row >= n_lat
        scale = jnp.where(is_ctx, sc_ref[1:2, :], sc_ref[0:1, :])
        shift = jnp.where(is_ctx, sh_ref[1:2, :], sh_ref[0:1, :])
        h_ref[...] = (y * g_ref[...] * (1.0 + scale) + shift).astype(BF16)

    acc = jnp.dot(h_ref[...], w_ref[...], preferred_element_type=F32)
    if epilogue == "relu2":
        o_ref[...] = jnp.square(jnp.maximum(acc, 0.0)).astype(o_ref.dtype)
    elif epilogue == "transpose":
        o_ref[...] = acc.T.astype(o_ref.dtype)
    elif epilogue == "rope":
        for gi in range(acc.shape[1] // LANE):
            sl = slice(gi * LANE, (gi + 1) * LANE)
            a = acc[:, sl]
            if gi % rope_every == rope_every - 1:
                a = _rope_lanes(a, c_ref[...], sa_ref[...], sb_ref[...])
            o_ref[:, sl] = (a * cs_ref[:, sl]).astype(o_ref.dtype)
    else:
        o_ref[...] = acc.astype(o_ref.dtype)


def proj(x, w, g, scale2, shift2, *, n_lat, tm, tn, out_dtype, epilogue="plain", xcb=0,
         colscale=None, tables=None, rope_every=1):
    m = x.shape[0]
    k, n = w.shape
    tm, tn = min(tm, m), min(tn, n)
    assert m % tm == 0 and n % tn == 0 and (xcb + 1) * k <= x.shape[1]
    row = pl.BlockSpec((1, k), lambda i, j: (0, 0))
    two = pl.BlockSpec((2, k), lambda i, j: (0, 0))
    in_specs = [pl.BlockSpec((tm, k), lambda i, j: (i, xcb)), row, two, two,
                pl.BlockSpec((k, tn), lambda i, j: (0, j))]
    args = [x, g.reshape(1, k), scale2, shift2, w]
    if epilogue == "rope":
        tab = pl.BlockSpec((tm, LANE), lambda i, j: (i, 0))
        in_specs += [pl.BlockSpec((1, tn), lambda i, j: (0, j)), tab, tab, tab]
        args += [colscale.reshape(1, n), *tables]
    if epilogue == "transpose":
        out_spec = pl.BlockSpec((tn, tm), lambda i, j: (j, i))
        out_shape = jax.ShapeDtypeStruct((n, m), out_dtype)
    else:
        out_spec = pl.BlockSpec((tm, tn), lambda i, j: (i, j))
        out_shape = jax.ShapeDtypeStruct((m, n), out_dtype)
    return pl.pallas_call(
        functools.partial(_proj_kernel, n_lat=n_lat, epilogue=epilogue, rope_every=rope_every),
        grid=(m // tm, n // tn),
        in_specs=in_specs,
        out_specs=out_spec,
        out_shape=out_shape,
        scratch_shapes=[pltpu.VMEM((tm, k), BF16)],
        compiler_params=_params(("parallel", "arbitrary")),
        name="proj_" + epilogue,
    )(*args)


def _mm_kernel(a_ref, w_ref, res_ref, gate_ref, bias_ref, o_ref, acc_ref, *, nk):
    kk = pl.program_id(2)

    @pl.when(kk == 0)
    def _():
        acc_ref[...] = jnp.zeros_like(acc_ref)

    acc_ref[...] += jnp.dot(a_ref[...].astype(BF16), w_ref[...].astype(BF16),
                            preferred_element_type=F32)

    @pl.when(kk == nk - 1)
    def _():
        o_ref[...] = (res_ref[...] + gate_ref[...] * acc_ref[...] + bias_ref[...]
                      ).astype(o_ref.dtype)


def matmul_epilogue(a, w, res, gate, bias, *, out_dtype=F32, tm=512, tn=1024, tk=2048):
    m, k = a.shape
    n = w.shape[1]
    tm, tn, tk = min(tm, m), min(tn, n), min(tk, k)
    assert m % tm == 0 and n % tn == 0 and k % tk == 0
    nk = k // tk
    return pl.pallas_call(
        functools.partial(_mm_kernel, nk=nk),
        grid=(m // tm, n // tn, nk),
        in_specs=[pl.BlockSpec((tm, tk), lambda i, j, kk: (i, kk)),
                  pl.BlockSpec((tk, tn), lambda i, j, kk: (kk, j)),
                  pl.BlockSpec((tm, tn), lambda i, j, kk: (i, j)),
                  pl.BlockSpec((1, tn), lambda i, j, kk: (0, j)),
                  pl.BlockSpec((1, tn), lambda i, j, kk: (0, j))],
        out_specs=pl.BlockSpec((tm, tn), lambda i, j, kk: (i, j)),
        out_shape=jax.ShapeDtypeStruct((m, n), out_dtype),
        scratch_shapes=[pltpu.VMEM((tm, tn), F32)],
        compiler_params=_params(("parallel", "parallel", "arbitrary")),
        name="matmul_epilogue",
    )(a, w, res, gate.reshape(1, n), bias.reshape(1, n))


def _flash_init(m_ref, acc_ref):
    m_ref[...] = jnp.full_like(m_ref, -jnp.inf)
    acc_ref[...] = jnp.zeros_like(acc_ref)


def _flash_streams(streams, s_ref, p_ref, m_ref, acc_ref):
    n = len(streams)
    tk = s_ref.shape[1]
    ones = jnp.ones((ONES_ROWS, tk), BF16)
    alpha = [None] * n
    vts = [None] * n
    for t in range(n + 2):
        if t < n:
            kx, qx, vts[t] = streams[t]()
            s_ref[t % 2] = lax.dot_general(kx, qx, (((1,), (1,)), ((), ())),
                                           preferred_element_type=F32)
        u = t - 1
        if 0 <= u < n:
            m_prev = m_ref[u]
            m_new = jnp.maximum(m_prev, jnp.max(s_ref[u % 2], axis=0, keepdims=True))
            m_ref[u] = m_new
            alpha[u] = jnp.exp2(m_prev - m_new)
            p_ref[u % 2] = jnp.exp2(s_ref[u % 2] - m_new).astype(BF16)
        w = t - 2
        if 0 <= w < n:
            vt_ext = jnp.concatenate([vts[w], ones], axis=0)
            acc_ref[w] = alpha[w] * acc_ref[w] + jnp.dot(vt_ext, p_ref[w % 2],
                                                         preferred_element_type=F32)


def _diff_flash_kernel(lamqk_ref, subg_ref, q1_ref, q2_ref, k1_ref, k2_ref, vt_ref, o_ref,
                       s_ref, p_ref, m_ref, acc_ref, *, nk, lam_init):
    j = pl.program_id(2)

    @pl.when(j == 0)
    def _():
        _flash_init(m_ref, acc_ref)

    lane = lax.broadcasted_iota(jnp.int32, q1_ref.shape, 1)

    def stream(q_ref, k_ref, half):
        def load():
            q = q_ref[...]
            qh = jnp.where(lane >= DA if half else lane < DA, q, jnp.zeros_like(q))
            return k_ref[...], qh, vt_ref[half * DV_A:(half + 1) * DV_A, :]
        return load

    _flash_streams([stream(q1_ref, k1_ref, 0), stream(q1_ref, k1_ref, 1),
                    stream(q2_ref, k2_ref, 0), stream(q2_ref, k2_ref, 1)],
                   s_ref, p_ref, m_ref, acc_ref)

    @pl.when(j == nk - 1)
    def _():
        lq = lamqk_ref[...]
        lam = (jnp.exp(jnp.sum(lq[0:1] * lq[1:2], axis=-1, keepdims=True))
               - jnp.exp(jnp.sum(lq[2:3] * lq[3:4], axis=-1, keepdims=True)) + lam_init)
        norm = lambda i: acc_ref[i, :DV_A, :] / acc_ref[i, DV_A:DV_A + 1, :]
        for half in range(2):
            o = norm(half) - lam * norm(2 + half)
            o = o * lax.rsqrt(jnp.mean(o * o, axis=0, keepdims=True) + EPS)
            o = o * subg_ref[...] * (1.0 - lam_init)
            o_ref[:, half * DV_A:(half + 1) * DV_A] = o.T.astype(o_ref.dtype)


def diff_flash_attention(lam_qk, subln_g, qk, vt, *, lam_init, n_q, q_row0, n_kv, kv_row0,
                         tq, tk):
    tq, tk = min(tq, n_q), min(tk, n_kv)
    assert n_q % tq == 0 and n_kv % tk == 0 and q_row0 % tq == 0 and kv_row0 % tk == 0
    nk = n_kv // tk
    qb, kb = q_row0 // tq, kv_row0 // tk
    npair = H_A // 2
    qspec = lambda c0: pl.BlockSpec((tq, LANE), lambda h, i, j: (i + qb, c0 + h))
    kspec = lambda c0: pl.BlockSpec((tk, LANE), lambda h, i, j: (j + kb, c0 + h))
    fixed = lambda h, i, j: (0, 0)
    return pl.pallas_call(
        functools.partial(_diff_flash_kernel, nk=nk, lam_init=lam_init),
        grid=(npair, n_q // tq, nk),
        in_specs=[pl.BlockSpec((4, DA), fixed), pl.BlockSpec((DV_A, 1), fixed),
                  qspec(0), qspec(npair), kspec(2 * npair), kspec(3 * npair),
                  pl.BlockSpec((2 * DV_A, tk), lambda h, i, j: (h, j + kb))],
        out_specs=pl.BlockSpec((tq, 2 * DV_A), lambda h, i, j: (i, h)),
        out_shape=jax.ShapeDtypeStruct((n_q, W_A), BF16),
        scratch_shapes=[pltpu.VMEM((2, tk, tq), F32), pltpu.VMEM((2, tk, tq), BF16),
                        pltpu.VMEM((4, 1, tq), F32),
                        pltpu.VMEM((4, DV_A + ONES_ROWS, tq), F32)],
        compiler_params=_params(("parallel", "parallel", "arbitrary")),
        name="diff_flash",
    )(lam_qk, subln_g.reshape(DV_A, 1), qk, qk, qk, qk, vt)


def _mla_flash_kernel(q_ref, kn_ref, kr_ref, vt_ref, o_ref, s_ref, p_ref, m_ref, acc_ref,
                      *, nk, hps):
    j = pl.program_id(2)

    @pl.when(j == 0)
    def _():
        _flash_init(m_ref, acc_ref)

    dk = 2 * LANE

    def stream(hh):
        def load():
            kx = jnp.concatenate([kn_ref[:, hh * D_NOPE:(hh + 1) * D_NOPE], kr_ref[...]], axis=1)
            return kx, q_ref[:, hh * dk:(hh + 1) * dk], vt_ref[hh * D_VC:(hh + 1) * D_VC, :]
        return load

    _flash_streams([stream(hh) for hh in range(hps)], s_ref, p_ref, m_ref, acc_ref)

    @pl.when(j == nk - 1)
    def _():
        for hh in range(hps):
            o = acc_ref[hh, :D_VC, :] / acc_ref[hh, D_VC:D_VC + 1, :]
            o_ref[:, hh * D_VC:(hh + 1) * D_VC] = o.T.astype(o_ref.dtype)


def mla_flash_attention(q, kn, kr, vt, *, n_q, q_row0, n_kv, kv_row0, tq, tk, hps=4):
    tq, tk = min(tq, n_q), min(tk, n_kv)
    assert n_q % tq == 0 and n_kv % tk == 0 and q_row0 % tq == 0 and kv_row0 % tk == 0
    assert H_C % hps == 0 and D_NOPE == LANE
    nk = n_kv // tk
    qb, kb = q_row0 // tq, kv_row0 // tk
    dk = 2 * LANE
    return pl.pallas_call(
        functools.partial(_mla_flash_kernel, nk=nk, hps=hps),
        grid=(H_C // hps, n_q // tq, nk),
        in_specs=[pl.BlockSpec((tq, hps * dk), lambda h, i, j: (i + qb, h)),
                  pl.BlockSpec((tk, hps * D_NOPE), lambda h, i, j: (j + kb, h)),
                  pl.BlockSpec((tk, LANE), lambda h, i, j: (j + kb, 0)),
                  pl.BlockSpec((hps * D_VC, tk), lambda h, i, j: (h, j + kb))],
        out_specs=pl.BlockSpec((tq, hps * D_VC), lambda h, i, j: (i, h)),
        out_shape=jax.ShapeDtypeStruct((n_q, W_C), BF16),
        scratch_shapes=[pltpu.VMEM((2, tk, tq), F32), pltpu.VMEM((2, tk, tq), BF16),
                        pltpu.VMEM((hps, 1, tq), F32),
                        pltpu.VMEM((hps, D_VC + ONES_ROWS, tq), F32)],
        compiler_params=_params(("parallel", "parallel", "arbitrary")),
        name="mla_flash",
    )(q, kn, kr, vt)


def _split3(a):
    a1 = a.astype(BF16)
    r1 = a - a1.astype(F32)
    a2 = r1.astype(BF16)
    a3 = (r1 - a2.astype(F32)).astype(BF16)
    return a1, a2, a3


def _ssd_kernel(xs_ref, b_ref, c_ref, aexp_ref, dtexp_ref, at_ref, y_ref, h_scr):
    d = pl.program_id(0)

    @pl.when(pl.program_id(1) == 0)
    def _():
        h_scr[...] = jnp.zeros_like(h_scr)

    row = lax.broadcasted_iota(jnp.int32, (CHUNK, CHUNK), 0)
    col = lax.broadcasted_iota(jnp.int32, (CHUNK, CHUNK), 1)
    mask = jnp.where(d == 0, row - col, col - row) >= 0
    tc = mask.astype(BF16)

    a_exp = aexp_ref[0]
    cum = sum(jnp.dot(tc, piece, preferred_element_type=F32) for piece in _split3(a_exp))
    total = jnp.sum(a_exp, axis=0, keepdims=True)
    cum_t = sum(lax.dot_general(piece, tc, (((1,), (1,)), ((), ())),
                                preferred_element_type=F32)
                for piece in _split3(at_ref[0]))

    xd = xs_ref[...] * dtexp_ref[0]
    xdd = (xd * jnp.exp(total - cum)).astype(BF16)
    xd_b = xd.astype(BF16)
    h_prev = h_scr[...]
    h_prev_b = h_prev.astype(BF16)
    lane = lax.broadcasted_iota(jnp.int32, (CHUNK, LANE), 1)
    gw = D_INNER // G_M

    y_parts, st_parts = [], []
    for g in range(G_M):
        bg = b_ref[:, g * N_STATE:(g + 1) * N_STATE]
        cg = c_ref[:, g * N_STATE:(g + 1) * N_STATE].astype(BF16)
        cb = lax.dot_general(cg, bg.astype(BF16), (((1,), (1,)), ((), ())),
                             preferred_element_type=F32)
        y_off = jnp.dot(cg, h_prev_b[:, g * gw:(g + 1) * gw], preferred_element_type=F32)
        st_parts.append(jnp.dot(bg.T.astype(BF16), xdd[:, g * gw:(g + 1) * gw],
                                preferred_element_type=F32))
        for pr in range(gw // LANE):
            lo = g * gw + pr * LANE
            xp = xd_b[:, lo:lo + LANE]
            halves = (jnp.where(lane < P_M, xp, 0), jnp.where(lane >= P_M, xp, 0))
            y_pair = y_off[:, pr * LANE:(pr + 1) * LANE] * jnp.exp(cum[:, lo:lo + LANE])
            for half in range(2):
                hh = lo // P_M + half
                seg = cum[:, hh * P_M:hh * P_M + 1] - cum_t[hh:hh + 1, :]
                mm = (jnp.exp(jnp.where(mask, seg, -jnp.inf)) * cb).astype(BF16)
                y_pair += jnp.dot(mm, halves[half], preferred_element_type=F32)
            y_parts.append(y_pair)

    y_ref[0] = jnp.concatenate(y_parts, axis=1)
    h_scr[...] = h_prev * jnp.exp(total) + jnp.concatenate(st_parts, axis=1)


def ssd_scan(xs, bm, cm, a_exp, dt_exp, a_t, *, n_lat):
    t_len = xs.shape[0]
    n_c, n_l = (t_len - n_lat) // CHUNK, n_lat // CHUNK

    def cidx(d, c):
        in_ctx = c < n_c
        cc = jnp.where(in_ctx, c, c - n_c)
        last = jnp.where(in_ctx, n_c - 1, n_l - 1)
        local = cc + d * (last - 2 * cc)
        return jnp.where(in_ctx, n_l + local, local)

    return pl.pallas_call(
        _ssd_kernel,
        grid=(2, n_c + n_l),
        in_specs=[pl.BlockSpec((CHUNK, D_INNER), lambda d, c: (cidx(d, c), 0)),
                  pl.BlockSpec((CHUNK, G_M * N_STATE), lambda d, c: (cidx(d, c), 0)),
                  pl.BlockSpec((CHUNK, G_M * N_STATE), lambda d, c: (cidx(d, c), 0)),
                  pl.BlockSpec((1, CHUNK, D_INNER), lambda d, c: (d, cidx(d, c), 0)),
                  pl.BlockSpec((1, CHUNK, D_INNER), lambda d, c: (d, cidx(d, c), 0)),
                  pl.BlockSpec((1, H_M, CHUNK), lambda d, c: (d, 0, cidx(d, c)))],
        out_specs=pl.BlockSpec((1, CHUNK, D_INNER), lambda d, c: (d, cidx(d, c), 0)),
        out_shape=jax.ShapeDtypeStruct((2, t_len, D_INNER), F32),
        scratch_shapes=[pltpu.VMEM((N_STATE, D_INNER), F32)],
        compiler_params=_params(("arbitrary", "arbitrary")),
        name="ssd_scan",
    )(xs, bm, cm, a_exp, dt_exp, a_t)


def _merge_kernel(oa_ref, y_ref, xs_ref, dsk_ref, z_ref,
                  sng_ref, oc_ref, w0_ref, w1_ref, w2_ref, g0_ref, g1_ref, g2_ref,
                  o_ref, br_ref):
    @pl.when(pl.program_id(1) == 0)
    def _():
        z = z_ref[...]
        u = (y_ref[0] + y_ref[1] + xs_ref[...] * dsk_ref[...]) * (z * jax.nn.sigmoid(z))
        gw = D_INNER // G_M
        for g in range(G_M):
            sl = slice(g * gw, (g + 1) * gw)
            ug = u[:, sl]
            ug = ug * lax.rsqrt(jnp.mean(ug * ug, axis=-1, keepdims=True) + EPS)
            br_ref[:, sl] = (ug * sng_ref[:, sl]).astype(BF16)

    acc = jax.nn.sigmoid(g0_ref[...]) * jnp.dot(oa_ref[...], w0_ref[...],
                                                preferred_element_type=F32)
    acc += jax.nn.sigmoid(g1_ref[...]) * jnp.dot(br_ref[...], w1_ref[...],
                                                 preferred_element_type=F32)
    acc += jax.nn.sigmoid(g2_ref[...]) * jnp.dot(oc_ref[...], w2_ref[...],
                                                 preferred_element_type=F32)
    o_ref[...] = acc.astype(o_ref.dtype)


def merge_branches(oa, oc, y, xs, ssm, gates, dskip_exp, ssm_norm_g, w0, w1, w2, *, row0,
                   tm=256, tn=512):
    m = oa.shape[0]
    n = D_MODEL
    tm = min(tm, m)
    assert m % tm == 0 and n % tn == 0 and row0 % tm == 0 and SSM_Z == 0
    rb = row0 // tm
    own = lambda i, j: (i, 0)
    tok = lambda i, j: (i + rb, 0)
    fixed = lambda i, j: (0, 0)
    wspec = pl.BlockSpec((W_A, tn), lambda i, j: (0, j))
    gspec = lambda b: pl.BlockSpec((tm, tn), lambda i, j: (i + rb, b * (n // tn) + j))
    return pl.pallas_call(
        _merge_kernel,
        grid=(m // tm, n // tn),
        in_specs=[pl.BlockSpec((tm, W_A), own),
                  pl.BlockSpec((2, tm, D_INNER), lambda i, j: (0, i + rb, 0)),
                  pl.BlockSpec((tm, D_INNER), tok),
                  pl.BlockSpec((1, D_INNER), fixed),
                  pl.BlockSpec((tm, D_INNER), tok),
                  pl.BlockSpec((1, D_INNER), fixed),
                  pl.BlockSpec((tm, W_C), own),
                  wspec, wspec, wspec, gspec(0), gspec(1), gspec(2)],
        out_specs=pl.BlockSpec((tm, tn), lambda i, j: (i, j)),
        out_shape=jax.ShapeDtypeStruct((m, n), BF16),
        scratch_shapes=[pltpu.VMEM((tm, D_INNER), BF16)],
        compiler_params=_params(("parallel", "arbitrary")),
        name="merge_branches",
    )(oa, y, xs, dskip_exp.reshape(1, D_INNER), ssm, ssm_norm_g.reshape(1, D_INNER), oc,
      w0, w1, w2, gates, gates, gates)


def _rmsnorm_kernel(x_ref, g_ref, o_ref):
    x = x_ref[...]
    o_ref[...] = x * lax.rsqrt(jnp.mean(x * x, axis=-1, keepdims=True) + EPS) * g_ref[...]


def rmsnorm_rows(x, g, *, tm=512):
    m, k = x.shape
    tm = min(tm, m)
    return pl.pallas_call(
        _rmsnorm_kernel,
        grid=(m // tm,),
        in_specs=[pl.BlockSpec((tm, k), lambda i: (i, 0)), pl.BlockSpec((1, k), lambda i: (0, 0))],
        out_specs=pl.BlockSpec((tm, k), lambda i: (i, 0)),
        out_shape=jax.ShapeDtypeStruct((m, k), F32),
        compiler_params=_params(("parallel",)),
        name="final_rmsnorm",
    )(x, g.reshape(1, k))


def _rope_tables(n_lat, n_ctx):
    rows = n_lat // GRID_W
    row = jnp.repeat(jnp.arange(rows), GRID_W).astype(F32)
    col = jnp.tile(jnp.arange(GRID_W), rows).astype(F32)
    n_freq = DA // 4
    inv = 1.0 / (ROPE_BASE ** (jnp.arange(n_freq, dtype=F32) / n_freq))
    ang = jnp.concatenate([row[:, None] * inv, col[:, None] * inv], axis=-1)
    reps = LANE // (DA // 2)
    cos, sin = jnp.tile(jnp.cos(ang), (1, reps)), jnp.tile(jnp.sin(ang), (1, reps))
    first = (jnp.arange(LANE) % DA) < DA // 2
    pad = lambda t, v: jnp.concatenate([t, jnp.full((n_ctx, LANE), v, F32)], axis=0)
    return (pad(cos, 1.0), pad(jnp.where(first, -sin, 0.0), 0.0),
            pad(jnp.where(first, 0.0, sin), 0.0))


def _kv_block(n_kv):
    return max(b for b in range(LANE, KV_BLOCK_MAX + 1, LANE) if n_kv % b == 0)


def _pad_last(t, width):
    return jnp.pad(t, [(0, 0)] * (t.ndim - 1) + [(0, width - t.shape[-1])])


def _layout_w_in(w):
    offs, acc = [], 0
    for sz in IN_SIZES[:-1]:
        acc += sz
        offs.append(acc)
    q, k, v, z, xbc, dt, cq, ckv, kr, gates = jnp.split(w, offs, axis=-1)
    cat = lambda *ts: jnp.concatenate(ts, axis=-1).astype(BF16)
    return dict(qk=cat(q, k), v=v.astype(BF16), ssm=cat(z, xbc, _pad_last(dt, LANE)),
                mla=cat(cq, ckv), kr=_pad_last(kr, LANE).astype(BF16), gates=gates.astype(BF16))


def _layout_w_uq(w):
    w = w.reshape(Q_LORA, H_C, D_NOPE + D_ROPE)
    return _pad_last(w, 2 * LANE).reshape(Q_LORA, H_C * 2 * LANE).astype(BF16)


def _layout_w_ukv(w):
    w = w.reshape(KV_LORA, H_C, D_NOPE + D_VC)
    return (w[..., :D_NOPE].reshape(KV_LORA, H_C * D_NOPE).astype(BF16),
            w[..., D_NOPE:].reshape(KV_LORA, H_C * D_VC).astype(BF16))


def _dwconv_silu(u, w, bias):
    pad = D_CONV // 2
    up = jnp.pad(u, ((pad, pad), (0, 0)))
    y = sum(up[i:i + u.shape[0]] * w[i] for i in range(D_CONV)) + bias
    return y * jax.nn.sigmoid(y)


def _ssm_operands(ssm, n_lat, conv_w, conv_b, dt_bias, a_neg):
    raw = ssm[:, SSM_XBC:SSM_XBC + CONV_CH]
    xbc = jnp.concatenate([_dwconv_silu(raw[:n_lat], conv_w, conv_b),
                           _dwconv_silu(raw[n_lat:], conv_w, conv_b)], axis=0)
    xs = xbc[:, :D_INNER]
    bm = xbc[:, D_INNER:D_INNER + G_M * N_STATE]
    cm = xbc[:, D_INNER + G_M * N_STATE:]
    dt_raw = ssm[:, SSM_DT:SSM_DT + 2 * H_M].reshape(-1, 2, H_M)
    dt = jnp.moveaxis(jax.nn.softplus(dt_raw + dt_bias[None]), 1, 0)
    a = dt * a_neg[:, None, :]
    return (xs, bm, cm, jnp.repeat(a, P_M, axis=-1), jnp.repeat(dt, P_M, axis=-1),
            jnp.swapaxes(a, 1, 2))


def kernel(x, c, ctx, c_ctx, norm1_g, norm2_g, w_ada, b_ada, w_in, lam_qk, subln_g, w_o_diff,
           conv_w, conv_b, a_log, dt_bias, d_skip, ssm_norm_g, w_o_ssm, q_norm_g, w_uq,
           kv_norm_g, w_ukv, w_o_mla, w_out, w_mlp1, w_mlp2, final_norm_g):
    n_lat, n_ctx = x.shape[1], ctx.shape[1]
    t_all = n_lat + n_ctx
    tables = _rope_tables(n_lat, n_ctx)
    xcat = jnp.concatenate([x[0], ctx[0]], axis=0)
    zero_d = jnp.zeros((D_MODEL,), F32)
    cs_diff = jnp.concatenate([jnp.full((2 * H_A * DA,), (DA ** -0.5) * LOG2E, F32),
                               jnp.ones((2 * H_A * DA,), F32)])
    cs_mla = jnp.full((H_C * 2 * LANE,), MLA_SCALE * LOG2E, F32)
    ones_l = jnp.ones((LANE,), F32)
    both = lambda v: jnp.stack([v, v])

    for l in range(DEPTH):
        last = l == DEPTH - 1
        lam_init = 0.8 - 0.6 * math.exp(-0.3 * l)
        a_neg = -jnp.exp(a_log[l].astype(F32))

        cond = jnp.concatenate([c, c_ctx[None], jnp.zeros((6, D_MODEL), F32)], axis=0)
        cond = cond * jax.nn.sigmoid(cond)
        nmod = N_MOD * D_MODEL
        mod = matmul_epilogue(cond, w_ada[l], jnp.zeros((8, nmod), F32), jnp.ones((nmod,), F32),
                              b_ada[l], tm=8, tn=1024, tk=D_MODEL)
        mx = [mod[0, i * D_MODEL:(i + 1) * D_MODEL] for i in range(N_MOD)]
        mc = [mod[1, i * D_MODEL:(i + 1) * D_MODEL] for i in range(N_MOD)]
        sc1, sh1 = mod[0:2, D_MODEL:2 * D_MODEL], mod[0:2, 0:D_MODEL]

        w_in_l = _layout_w_in(w_in[l])
        w_uq_l = _layout_w_uq(w_uq[l])
        w_uk_l, w_uv_l = _layout_w_ukv(w_ukv[l])
        w_od, w_os, w_om = (w.astype(BF16) for w in (w_o_diff[l], w_o_ssm[l], w_o_mla[l]))
        w_out_l, w1_l, w2_l = (w.astype(BF16) for w in (w_out[l], w_mlp1[l], w_mlp2[l]))
        dskip_exp = jnp.repeat(d_skip[l], P_M)
        z2q, z2kv = jnp.zeros((2, Q_LORA), F32), jnp.zeros((2, KV_LORA), F32)

        inp = functools.partial(proj, xcat, g=norm1_g[l], scale2=sc1, shift2=sh1,
                                n_lat=n_lat, tm=640)
        qk = inp(w_in_l["qk"], tn=512, out_dtype=BF16, epilogue="rope", colscale=cs_diff,
                 tables=tables)
        vt = inp(w_in_l["v"], tn=512, out_dtype=BF16, epilogue="transpose")
        ssm = inp(w_in_l["ssm"], tn=896, out_dtype=F32)
        mla = inp(w_in_l["mla"], tn=768, out_dtype=F32)
        kr = inp(w_in_l["kr"], tn=LANE, out_dtype=BF16, epilogue="rope", colscale=ones_l,
                 tables=tables)
        gates = inp(w_in_l["gates"], tn=1024, out_dtype=F32)

        up = functools.partial(proj, mla, n_lat=n_lat, tm=640)
        mq = up(w_uq_l, q_norm_g[l], z2q, z2q, xcb=0, tn=1024, out_dtype=BF16, epilogue="rope",
                colscale=cs_mla, tables=tables, rope_every=2)
        kn = up(w_uk_l, kv_norm_g[l], z2kv, z2kv, xcb=Q_LORA // KV_LORA, tn=1024, out_dtype=BF16)
        mvt = up(w_uv_l, kv_norm_g[l], z2kv, z2kv, xcb=Q_LORA // KV_LORA, tn=512,
                 out_dtype=BF16, epilogue="transpose")

        xs, bm, cm, a_exp, dt_exp, a_t = _ssm_operands(ssm, n_lat, conv_w[l], conv_b[l],
                                                       dt_bias[l], a_neg)
        y = ssd_scan(xs, bm, cm, a_exp, dt_exp, a_t, n_lat=n_lat)

        def tail(n_q, q_row0, n_kv, kv_row0, res, m, tq, tk, tm):
            oa = diff_flash_attention(lam_qk[l], subln_g[l], qk, vt, lam_init=lam_init, n_q=n_q,
                                      q_row0=q_row0, n_kv=n_kv, kv_row0=kv_row0, tq=tq, tk=tk)
            oc = mla_flash_attention(mq, kn, kr, mvt, n_q=n_q, q_row0=q_row0, n_kv=n_kv,
                                     kv_row0=kv_row0, tq=tq, tk=tk)
            merged = merge_branches(oa, oc, y, xs, ssm, gates, dskip_exp, ssm_norm_g[l],
                                    w_od, w_os, w_om, row0=q_row0)
            s1 = matmul_epilogue(merged, w_out_l, res, m[2], zero_d, tm=tm, tn=1024, tk=D_MODEL)
            hmid = proj(s1, w1_l, norm2_g[l], both(m[4]), both(m[3]), n_lat=n_q, tm=tm, tn=1024,
                        out_dtype=BF16, epilogue="relu2")
            return matmul_epilogue(hmid, w2_l, s1, m[5], zero_d, tm=tm, tn=1024, tk=2048)

        new_lat = tail(n_lat, 0, t_all, 0, xcat, mx, 512, _kv_block(t_all), 512)
        if not last:
            new_cx = tail(n_ctx, n_lat, n_ctx, n_lat, xcat[n_lat:], mc, n_ctx, n_ctx, n_ctx)
            xcat = jnp.concatenate([new_lat, new_cx], axis=0)

    return rmsnorm_rows(new_lat, final_norm_g)[None]
```

```python
import functools
import math

import jax
import jax.numpy as jnp
from jax import lax
from jax.experimental import pallas as pl
from jax.experimental.pallas import tpu as pltpu

D_MODEL = 2048
DEPTH = 2
GRID_W = 64
ROPE_BASE = 10000.0
EPS = 1e-6

H_A = 8
DA = 64
DV_A = 2 * DA
W_A = H_A * DV_A

H_M = 16
P_M = 64
D_INNER = H_M * P_M
N_STATE = 128
G_M = 2
D_CONV = 5
CHUNK = 128
CONV_CH = D_INNER + 2 * G_M * N_STATE

H_C = 8
D_NOPE = 128
D_ROPE = 64
D_VC = 128
Q_LORA = 512
KV_LORA = 256
W_C = H_C * D_VC
MLA_SCALE = (D_NOPE + D_ROPE) ** -0.5

D_FF = 4 * D_MODEL
N_BRANCH = 3
N_MOD = 6

IN_SIZES = (2 * H_A * DA, 2 * H_A * DA, W_A, D_INNER, CONV_CH, 2 * H_M,
            Q_LORA, KV_LORA, D_ROPE, N_BRANCH * D_MODEL)

LANE = 128
ONES_ROWS = 16
KV_BLOCK_MAX = 3328
LOG2E = 1.4426950408889634
VMEM_LIMIT = 48 * 1024 * 1024

BF16 = jnp.bfloat16
F32 = jnp.float32

assert DA == D_ROPE and LANE % DA == 0


def _params(sem):
    return pltpu.CompilerParams(dimension_semantics=sem, vmem_limit_bytes=VMEM_LIMIT)


def _rope_lanes(a, c, sa, sb):
    return a * c + pltpu.roll(a, LANE - DA // 2, 1) * sa + pltpu.roll(a, DA // 2, 1) * sb


def _norm_rows(x_ref, g_ref, sc_ref, sh_ref, row0, n_lat):
    x = x_ref[...]
    y = x * lax.rsqrt(jnp.mean(x * x, axis=-1, keepdims=True) + EPS)
    row = row0 + lax.broadcasted_iota(jnp.int32, (x.shape[0], 1), 0)
    is_ctx = row >= n_lat
    scale = jnp.where(is_ctx, sc_ref[1:2, :], sc_ref[0:1, :])
    shift = jnp.where(is_ctx, sh_ref[1:2, :], sh_ref[0:1, :])
    return (y * g_ref[...] * (1.0 + scale) + shift).astype(BF16)


def _rownorm_kernel(x_ref, g_ref, sc_ref, sh_ref, o_ref, *, n_lat):
    o_ref[...] = _norm_rows(x_ref, g_ref, sc_ref, sh_ref,
                            pl.program_id(0) * x_ref.shape[0], n_lat)


def rownorm(x, g, scale2, shift2, *, n_lat, tm):
    m, k = x.shape
    assert m % tm == 0
    two = pl.BlockSpec((2, k), lambda i: (0, 0))
    return pl.pallas_call(
        functools.partial(_rownorm_kernel, n_lat=n_lat),
        grid=(m // tm,),
        in_specs=[pl.BlockSpec((tm, k), lambda i: (i, 0)),
                  pl.BlockSpec((1, k), lambda i: (0, 0)), two, two],
        out_specs=pl.BlockSpec((tm, k), lambda i: (i, 0)),
        out_shape=jax.ShapeDtypeStruct((m, k), BF16),
        compiler_params=_params(("parallel",)),
        name="rownorm",
    )(x, g.reshape(1, k), scale2, shift2)


def _proj_kernel(*refs, n_lat, epilogue, rope_every, fused_norm):
    refs = list(refs)
    x_ref = refs.pop(0)
    if fused_norm:
        g_ref, sc_ref, sh_ref = refs.pop(0), refs.pop(0), refs.pop(0)
        h_ref = refs.pop()
        row0 = pl.program_id(0) * x_ref.shape[0]

        @pl.when(pl.program_id(1) == 0)
        def _():
            h_ref[...] = _norm_rows(x_ref, g_ref, sc_ref, sh_ref, row0, n_lat)
    else:
        h_ref = x_ref
    if epilogue == "rope":
        w_ref, cs_ref, c_ref, sa_ref, sb_ref, o_ref = refs
    else:
        w_ref, o_ref = refs

    acc = jnp.dot(h_ref[...], w_ref[...], preferred_element_type=F32)
    if epilogue == "relu2":
        o_ref[...] = jnp.square(jnp.maximum(acc, 0.0)).astype(o_ref.dtype)
    elif epilogue == "transpose":
        o_ref[...] = acc.T.astype(o_ref.dtype)
    elif epilogue == "rope":
        for gi in range(acc.shape[1] // LANE):
            sl = slice(gi * LANE, (gi + 1) * LANE)
            a = acc[:, sl]
            if gi % rope_every == rope_every - 1:
                a = _rope_lanes(a, c_ref[...], sa_ref[...], sb_ref[...])
            o_ref[:, sl] = (a * cs_ref[:, sl]).astype(o_ref.dtype)
    else:
        o_ref[...] = acc.astype(o_ref.dtype)


def proj(x, w, g=None, scale2=None, shift2=None, *, tm, tn, out_dtype, n_lat=0,
         epilogue="plain", xcb=0, colscale=None, tables=None, rope_every=1):
    m = x.shape[0]
    k, n = w.shape
    tm, tn = min(tm, m), min(tn, n)
    fused_norm = g is not None
    assert m % tm == 0 and n % tn == 0 and (xcb + 1) * k <= x.shape[1]
    assert fused_norm or (x.dtype == BF16 and x.shape[1] == k)
    in_specs = [pl.BlockSpec((tm, k), lambda i, j: (i, xcb))]
    args = [x]
    if fused_norm:
        two = pl.BlockSpec((2, k), lambda i, j: (0, 0))
        in_specs += [pl.BlockSpec((1, k), lambda i, j: (0, 0)), two, two]
        args += [g.reshape(1, k), scale2, shift2]
    in_specs.append(pl.BlockSpec((k, tn), lambda i, j: (0, j)))
    args.append(w)
    if epilogue == "rope":
        tab = pl.BlockSpec((tm, LANE), lambda i, j: (i, 0))
        in_specs += [pl.BlockSpec((1, tn), lambda i, j: (0, j)), tab, tab, tab]
        args += [colscale.reshape(1, n), *tables]
    if epilogue == "transpose":
        out_spec = pl.BlockSpec((tn, tm), lambda i, j: (j, i))
        out_shape = jax.ShapeDtypeStruct((n, m), out_dtype)
    else:
        out_spec = pl.BlockSpec((tm, tn), lambda i, j: (i, j))
        out_shape = jax.ShapeDtypeStruct((m, n), out_dtype)
    return pl.pallas_call(
        functools.partial(_proj_kernel, n_lat=n_lat, epilogue=epilogue, rope_every=rope_every,
                          fused_norm=fused_norm),
        grid=(m // tm, n // tn),
        in_specs=in_specs,
        out_specs=out_spec,
        out_shape=out_shape,
        scratch_shapes=[pltpu.VMEM((tm, k), BF16)] if fused_norm else [],
        compiler_params=_params(("parallel", "arbitrary")),
        name="proj_" + epilogue,
    )(*args)


def _mm_kernel(a_ref, w_ref, res_ref, gate_ref, bias_ref, o_ref, acc_ref, *, nk):
    kk = pl.program_id(2)

    @pl.when(kk == 0)
    def _():
        acc_ref[...] = jnp.zeros_like(acc_ref)

    acc_ref[...] += jnp.dot(a_ref[...].astype(BF16), w_ref[...].astype(BF16),
                            preferred_element_type=F32)

    @pl.when(kk == nk - 1)
    def _():
        o_ref[...] = (res_ref[...] + gate_ref[...] * acc_ref[...] + bias_ref[...]
                      ).astype(o_ref.dtype)


def matmul_epilogue(a, w, res, gate, bias, *, out_dtype=F32, tm=512, tn=1024, tk=2048):
    m, k = a.shape
    n = w.shape[1]
    tm, tn, tk = min(tm, m), min(tn, n), min(tk, k)
    assert m % tm == 0 and n % tn == 0 and k % tk == 0
    nk = k // tk
    return pl.pallas_call(
        functools.partial(_mm_kernel, nk=nk),
        grid=(m // tm, n // tn, nk),
        in_specs=[pl.BlockSpec((tm, tk), lambda i, j, kk: (i, kk)),
                  pl.BlockSpec((tk, tn), lambda i, j, kk: (kk, j)),
                  pl.BlockSpec((tm, tn), lambda i, j, kk: (i, j)),
                  pl.BlockSpec((1, tn), lambda i, j, kk: (0, j)),
                  pl.BlockSpec((1, tn), lambda i, j, kk: (0, j))],
        out_specs=pl.BlockSpec((tm, tn), lambda i, j, kk: (i, j)),
        out_shape=jax.ShapeDtypeStruct((m, n), out_dtype),
        scratch_shapes=[pltpu.VMEM((tm, tn), F32)],
        compiler_params=_params(("parallel", "parallel", "arbitrary")),
        name="matmul_epilogue",
    )(a, w, res, gate.reshape(1, n), bias.reshape(1, n))


def _flash_init(m_ref, acc_ref):
    m_ref[...] = jnp.full_like(m_ref, -jnp.inf)
    acc_ref[...] = jnp.zeros_like(acc_ref)


def _flash_streams(streams, s_ref, p_ref, m_ref, acc_ref):
    n = len(streams)
    tk = s_ref.shape[1]
    ones = jnp.ones((ONES_ROWS, tk), BF16)
    alpha = [None] * n
    vts = [None] * n
    for t in range(n + 2):
        if t < n:
            kx, qx, vts[t] = streams[t]()
            s_ref[t % 2] = lax.dot_general(kx, qx, (((1,), (1,)), ((), ())),
                                           preferred_element_type=F32)
        u = t - 1
        if 0 <= u < n:
            m_prev = m_ref[u]
            m_new = jnp.maximum(m_prev, jnp.max(s_ref[u % 2], axis=0, keepdims=True))
            m_ref[u] = m_new
            alpha[u] = jnp.exp2(m_prev - m_new)
            p_ref[u % 2] = jnp.exp2(s_ref[u % 2] - m_new).astype(BF16)
        w = t - 2
        if 0 <= w < n:
            vt_ext = jnp.concatenate([vts[w], ones], axis=0)
            acc_ref[w] = alpha[w] * acc_ref[w] + jnp.dot(vt_ext, p_ref[w % 2],
                                                         preferred_element_type=F32)


def _diff_flash_kernel(lamqk_ref, subg_ref, q1_ref, q2_ref, k1_ref, k2_ref, vt_ref, o_ref,
                       s_ref, p_ref, m_ref, acc_ref, *, nk, lam_init):
    j = pl.program_id(2)

    @pl.when(j == 0)
    def _():
        _flash_init(m_ref, acc_ref)

    lane = lax.broadcasted_iota(jnp.int32, q1_ref.shape, 1)

    def stream(q_ref, k_ref, half):
        def load():
            q = q_ref[...]
            qh = jnp.where(lane >= DA if half else lane < DA, q, jnp.zeros_like(q))
            return k_ref[...], qh, vt_ref[half * DV_A:(half + 1) * DV_A, :]
        return load

    _flash_streams([stream(q1_ref, k1_ref, 0), stream(q1_ref, k1_ref, 1),
                    stream(q2_ref, k2_ref, 0), stream(q2_ref, k2_ref, 1)],
                   s_ref, p_ref, m_ref, acc_ref)

    @pl.when(j == nk - 1)
    def _():
        lq = lamqk_ref[...]
        lam = (jnp.exp(jnp.sum(lq[0:1] * lq[1:2], axis=-1, keepdims=True))
               - jnp.exp(jnp.sum(lq[2:3] * lq[3:4], axis=-1, keepdims=True)) + lam_init)
        norm = lambda i: acc_ref[i, :DV_A, :] / acc_ref[i, DV_A:DV_A + 1, :]
        for half in range(2):
            o = norm(half) - lam * norm(2 + half)
            o = o * lax.rsqrt(jnp.mean(o * o, axis=0, keepdims=True) + EPS)
            o = o * subg_ref[...] * (1.0 - lam_init)
            o_ref[:, half * DV_A:(half + 1) * DV_A] = o.T.astype(o_ref.dtype)


def diff_flash_attention(lam_qk, subln_g, qk, vt, *, lam_init, n_q, q_row0, n_kv, kv_row0,
                         tq, tk):
    tq, tk = min(tq, n_q), min(tk, n_kv)
    assert n_q % tq == 0 and n_kv % tk == 0 and q_row0 % tq == 0 and kv_row0 % tk == 0
    nk = n_kv // tk
    qb, kb = q_row0 // tq, kv_row0 // tk
    npair = H_A // 2
    qspec = lambda c0: pl.BlockSpec((tq, LANE), lambda h, i, j: (i + qb, c0 + h))
    kspec = lambda c0: pl.BlockSpec((tk, LANE), lambda h, i, j: (j + kb, c0 + h))
    fixed = lambda h, i, j: (0, 0)
    return pl.pallas_call(
        functools.partial(_diff_flash_kernel, nk=nk, lam_init=lam_init),
        grid=(npair, n_q // tq, nk),
        in_specs=[pl.BlockSpec((4, DA), fixed), pl.BlockSpec((DV_A, 1), fixed),
                  qspec(0), qspec(npair), kspec(2 * npair), kspec(3 * npair),
                  pl.BlockSpec((2 * DV_A, tk), lambda h, i, j: (h, j + kb))],
        out_specs=pl.BlockSpec((tq, 2 * DV_A), lambda h, i, j: (i, h)),
        out_shape=jax.ShapeDtypeStruct((n_q, W_A), BF16),
        scratch_shapes=[pltpu.VMEM((2, tk, tq), F32), pltpu.VMEM((2, tk, tq), BF16),
                        pltpu.VMEM((4, 1, tq), F32),
                        pltpu.VMEM((4, DV_A + ONES_ROWS, tq), F32)],
        compiler_params=_params(("parallel", "parallel", "arbitrary")),
        name="diff_flash",
    )(lam_qk, subln_g.reshape(DV_A, 1), qk, qk, qk, qk, vt)


def _mla_flash_kernel(q_ref, kn_ref, kr_ref, vt_ref, o_ref, s_ref, p_ref, m_ref, acc_ref,
                      *, nk, hps):
    j = pl.program_id(2)

    @pl.when(j == 0)
    def _():
        _flash_init(m_ref, acc_ref)

    dk = 2 * LANE

    def stream(hh):
        def load():
            kx = jnp.concatenate([kn_ref[:, hh * D_NOPE:(hh + 1) * D_NOPE], kr_ref[...]], axis=1)
            return kx, q_ref[:, hh * dk:(hh + 1) * dk], vt_ref[hh * D_VC:(hh + 1) * D_VC, :]
        return load

    _flash_streams([stream(hh) for hh in range(hps)], s_ref, p_ref, m_ref, acc_ref)

    @pl.when(j == nk - 1)
    def _():
        for hh in range(hps):
            o = acc_ref[hh, :D_VC, :] / acc_ref[hh, D_VC:D_VC + 1, :]
            o_ref[:, hh * D_VC:(hh + 1) * D_VC] = o.T.astype(o_ref.dtype)


def mla_flash_attention(q, kn, kr, vt, *, n_q, q_row0, n_kv, kv_row0, tq, tk, hps=4):
    tq, tk = min(tq, n_q), min(tk, n_kv)
    assert n_q % tq == 0 and n_kv % tk == 0 and q_row0 % tq == 0 and kv_row0 % tk == 0
    assert H_C % hps == 0 and D_NOPE == LANE
    nk = n_kv // tk
    qb, kb = q_row0 // tq, kv_row0 // tk
    dk = 2 * LANE
    return pl.pallas_call(
        functools.partial(_mla_flash_kernel, nk=nk, hps=hps),
        grid=(H_C // hps, n_q // tq, nk),
        in_specs=[pl.BlockSpec((tq, hps * dk), lambda h, i, j: (i + qb, h)),
                  pl.BlockSpec((tk, hps * D_NOPE), lambda h, i, j: (j + kb, h)),
                  pl.BlockSpec((tk, LANE), lambda h, i, j: (j + kb, 0)),
                  pl.BlockSpec((hps * D_VC, tk), lambda h, i, j: (h, j + kb))],
        out_specs=pl.BlockSpec((tq, hps * D_VC), lambda h, i, j: (i, h)),
        out_shape=jax.ShapeDtypeStruct((n_q, W_C), BF16),
        scratch_shapes=[pltpu.VMEM((2, tk, tq), F32), pltpu.VMEM((2, tk, tq), BF16),
                        pltpu.VMEM((hps, 1, tq), F32),
                        pltpu.VMEM((hps, D_VC + ONES_ROWS, tq), F32)],
        compiler_params=_params(("parallel", "parallel", "arbitrary")),
        name="mla_flash",
    )(q, kn, kr, vt)


def _split3(a):
    a1 = a.astype(BF16)
    r1 = a - a1.astype(F32)
    a2 = r1.astype(BF16)
    a3 = (r1 - a2.astype(F32)).astype(BF16)
    return a1, a2, a3


def _ssm_prep_kernel(cur_ref, prev_ref, next_ref, cw_ref, cb_ref, dt_ref, dtb_ref, arow_ref,
                     xs_ref, bc_ref, aexp_ref, dtexp_ref, at_ref, ext_ref, *, n_l, n_c):
    c = pl.program_id(0)
    first = jnp.logical_or(c == 0, c == n_l)
    last = jnp.logical_or(c == n_l - 1, c == n_l + n_c - 1)
    halo = prev_ref.shape[0]
    ext_ref[0:halo, :] = jnp.where(first, 0.0, prev_ref[...])
    ext_ref[halo:halo + CHUNK, :] = cur_ref[...]
    ext_ref[halo + CHUNK:, :] = jnp.where(last, 0.0, next_ref[...])
    y = cb_ref[...]
    for i in range(D_CONV):
        r0 = halo - D_CONV // 2 + i
        y = y + ext_ref[r0:r0 + CHUNK, :] * cw_ref[i:i + 1, :]
    act = y * jax.nn.sigmoid(y)
    xs_ref[...] = act[:, :D_INNER]
    bc_ref[...] = act[:, D_INNER:]

    v = dt_ref[...] + dtb_ref[...]
    dt = jnp.maximum(v, 0.0) + jnp.log(1.0 + jnp.exp(-jnp.abs(v)))
    a = dt * arow_ref[...]
    at_ref[...] = a.T
    low = lax.broadcasted_iota(jnp.int32, (CHUNK, LANE), 1) < P_M
    for pr in range(2 * H_M // 2):
        h0, sl = 2 * pr, slice(pr * LANE, (pr + 1) * LANE)
        dtexp_ref[:, sl] = jnp.where(low, dt[:, h0:h0 + 1], dt[:, h0 + 1:h0 + 2])
        aexp_ref[:, sl] = jnp.where(low, a[:, h0:h0 + 1], a[:, h0 + 1:h0 + 2])


def ssm_prep(xbc, zdt, conv_w, conv_b, dt_bias, a_neg, *, n_lat):
    t_len = xbc.shape[0]
    n_l, n_c = n_lat // CHUNK, (t_len - n_lat) // CHUNK
    halo = 8
    per = CHUNK // halo
    nb = t_len // halo
    row = lambda v: _pad_last(v.reshape(1, -1), LANE)
    one = lambda shape: pl.BlockSpec(shape, lambda c: (0, 0))
    return pl.pallas_call(
        functools.partial(_ssm_prep_kernel, n_l=n_l, n_c=n_c),
        grid=(t_len // CHUNK,),
        in_specs=[pl.BlockSpec((CHUNK, CONV_CH), lambda c: (c, 0)),
                  pl.BlockSpec((halo, CONV_CH), lambda c: (jnp.maximum(c * per - 1, 0), 0)),
                  pl.BlockSpec((halo, CONV_CH), lambda c: (jnp.minimum((c + 1) * per, nb - 1), 0)),
                  one((D_CONV, CONV_CH)), one((1, CONV_CH)),
                  pl.BlockSpec((CHUNK, LANE), lambda c: (c, D_INNER // LANE)),
                  one((1, LANE)), one((1, LANE))],
        out_specs=[pl.BlockSpec((CHUNK, D_INNER), lambda c: (c, 0)),
                   pl.BlockSpec((CHUNK, 2 * G_M * N_STATE), lambda c: (c, 0)),
                   pl.BlockSpec((CHUNK, 2 * D_INNER), lambda c: (c, 0)),
                   pl.BlockSpec((CHUNK, 2 * D_INNER), lambda c: (c, 0)),
                   pl.BlockSpec((LANE, CHUNK), lambda c: (0, c))],
        out_shape=[jax.ShapeDtypeStruct((t_len, D_INNER), F32),
                   jax.ShapeDtypeStruct((t_len, 2 * G_M * N_STATE), F32),
                   jax.ShapeDtypeStruct((t_len, 2 * D_INNER), F32),
                   jax.ShapeDtypeStruct((t_len, 2 * D_INNER), F32),
                   jax.ShapeDtypeStruct((LANE, t_len), F32)],
        scratch_shapes=[pltpu.VMEM((CHUNK + 2 * halo, CONV_CH), F32)],
        compiler_params=_params(("parallel",)),
        name="ssm_prep",
    )(xbc, xbc, xbc, conv_w, conv_b.reshape(1, CONV_CH), zdt, row(dt_bias), row(a_neg))


def _ssd_kernel(xs_ref, bc_ref, aexp_ref, dtexp_ref, at_ref, y_ref, h_scr):
    d = pl.program_id(0)

    @pl.when(pl.program_id(1) == 0)
    def _():
        h_scr[...] = jnp.zeros_like(h_scr)

    row = lax.broadcasted_iota(jnp.int32, (CHUNK, CHUNK), 0)
    col = lax.broadcasted_iota(jnp.int32, (CHUNK, CHUNK), 1)
    mask = jnp.where(d == 0, row - col, col - row) >= 0
    tc = mask.astype(BF16)

    a_exp = aexp_ref[...]
    cum = sum(jnp.dot(tc, piece, preferred_element_type=F32) for piece in _split3(a_exp))
    total = jnp.sum(a_exp, axis=0, keepdims=True)
    a_t = jnp.where(d == 0, at_ref[0:H_M, :], at_ref[H_M:2 * H_M, :])
    cum_t = sum(lax.dot_general(piece, tc, (((1,), (1,)), ((), ())),
                                preferred_element_type=F32)
                for piece in _split3(a_t))

    xd = xs_ref[...] * dtexp_ref[...]
    xdd = (xd * jnp.exp(total - cum)).astype(BF16)
    xd_b = xd.astype(BF16)
    h_prev = h_scr[...]
    h_prev_b = h_prev.astype(BF16)
    lane = lax.broadcasted_iota(jnp.int32, (CHUNK, LANE), 1)
    gw = D_INNER // G_M

    y_parts, st_parts = [], []
    for g in range(G_M):
        bg = bc_ref[:, g * N_STATE:(g + 1) * N_STATE]
        cg = bc_ref[:, (G_M + g) * N_STATE:(G_M + g + 1) * N_STATE].astype(BF16)
        cb = lax.dot_general(cg, bg.astype(BF16), (((1,), (1,)), ((), ())),
                             preferred_element_type=F32)
        y_off = jnp.dot(cg, h_prev_b[:, g * gw:(g + 1) * gw], preferred_element_type=F32)
        st_parts.append(jnp.dot(bg.T.astype(BF16), xdd[:, g * gw:(g + 1) * gw],
                                preferred_element_type=F32))
        for pr in range(gw // LANE):
            lo = g * gw + pr * LANE
            xp = xd_b[:, lo:lo + LANE]
            halves = (jnp.where(lane < P_M, xp, 0), jnp.where(lane >= P_M, xp, 0))
            y_pair = y_off[:, pr * LANE:(pr + 1) * LANE] * jnp.exp(cum[:, lo:lo + LANE])
            for half in range(2):
                hh = lo // P_M + half
                seg = cum[:, hh * P_M:hh * P_M + 1] - cum_t[hh:hh + 1, :]
                mm = (jnp.exp(jnp.where(mask, seg, -jnp.inf)) * cb).astype(BF16)
                y_pair += jnp.dot(mm, halves[half], preferred_element_type=F32)
            y_parts.append(y_pair)

    y_ref[0] = jnp.concatenate(y_parts, axis=1)
    h_scr[...] = h_prev * jnp.exp(total) + jnp.concatenate(st_parts, axis=1)


def ssd_scan(xs, bc, a_exp, dt_exp, a_t, *, n_lat):
    t_len = xs.shape[0]
    n_c, n_l = (t_len - n_lat) // CHUNK, n_lat // CHUNK

    def cidx(d, c):
        in_ctx = c < n_c
        cc = jnp.where(in_ctx, c, c - n_c)
        last = jnp.where(in_ctx, n_c - 1, n_l - 1)
        local = cc + d * (last - 2 * cc)
        return jnp.where(in_ctx, n_l + local, local)

    return pl.pallas_call(
        _ssd_kernel,
        grid=(2, n_c + n_l),
        in_specs=[pl.BlockSpec((CHUNK, D_INNER), lambda d, c: (cidx(d, c), 0)),
                  pl.BlockSpec((CHUNK, 2 * G_M * N_STATE), lambda d, c: (cidx(d, c), 0)),
                  pl.BlockSpec((CHUNK, D_INNER), lambda d, c: (cidx(d, c), d)),
                  pl.BlockSpec((CHUNK, D_INNER), lambda d, c: (cidx(d, c), d)),
                  pl.BlockSpec((LANE, CHUNK), lambda d, c: (0, cidx(d, c)))],
        out_specs=pl.BlockSpec((1, CHUNK, D_INNER), lambda d, c: (d, cidx(d, c), 0)),
        out_shape=jax.ShapeDtypeStruct((2, t_len, D_INNER), F32),
        scratch_shapes=[pltpu.VMEM((N_STATE, D_INNER), F32)],
        compiler_params=_params(("arbitrary", "arbitrary")),
        name="ssd_scan",
    )(xs, bc, a_exp, dt_exp, a_t)


def _merge_kernel(oa_ref, y_ref, xs_ref, dsk_ref, z_ref,
                  sng_ref, oc_ref, w0_ref, w1_ref, w2_ref, g0_ref, g1_ref, g2_ref,
                  o_ref, br_ref):
    @pl.when(pl.program_id(1) == 0)
    def _():
        z = z_ref[...]
        u = (y_ref[0] + y_ref[1] + xs_ref[...] * dsk_ref[...]) * (z * jax.nn.sigmoid(z))
        gw = D_INNER // G_M
        for g in range(G_M):
            sl = slice(g * gw, (g + 1) * gw)
            ug = u[:, sl]
            ug = ug * lax.rsqrt(jnp.mean(ug * ug, axis=-1, keepdims=True) + EPS)
            br_ref[:, sl] = (ug * sng_ref[:, sl]).astype(BF16)

    acc = jax.nn.sigmoid(g0_ref[...]) * jnp.dot(oa_ref[...], w0_ref[...],
                                                preferred_element_type=F32)
    acc += jax.nn.sigmoid(g1_ref[...]) * jnp.dot(br_ref[...], w1_ref[...],
                                                 preferred_element_type=F32)
    acc += jax.nn.sigmoid(g2_ref[...]) * jnp.dot(oc_ref[...], w2_ref[...],
                                                 preferred_element_type=F32)
    o_ref[...] = acc.astype(o_ref.dtype)


def merge_branches(oa, oc, y, xs, zdt, gates, dskip_exp, ssm_norm_g, w0, w1, w2, *, row0,
                   tm=256, tn=512):
    m = oa.shape[0]
    n = D_MODEL
    tm = min(tm, m)
    assert m % tm == 0 and n % tn == 0 and row0 % tm == 0
    rb = row0 // tm
    own = lambda i, j: (i, 0)
    tok = lambda i, j: (i + rb, 0)
    fixed = lambda i, j: (0, 0)
    wspec = pl.BlockSpec((W_A, tn), lambda i, j: (0, j))
    gspec = lambda b: pl.BlockSpec((tm, tn), lambda i, j: (i + rb, b * (n // tn) + j))
    return pl.pallas_call(
        _merge_kernel,
        grid=(m // tm, n // tn),
        in_specs=[pl.BlockSpec((tm, W_A), own),
                  pl.BlockSpec((2, tm, D_INNER), lambda i, j: (0, i + rb, 0)),
                  pl.BlockSpec((tm, D_INNER), tok),
                  pl.BlockSpec((1, D_INNER), fixed),
                  pl.BlockSpec((tm, D_INNER), tok),
                  pl.BlockSpec((1, D_INNER), fixed),
                  pl.BlockSpec((tm, W_C), own),
                  wspec, wspec, wspec, gspec(0), gspec(1), gspec(2)],
        out_specs=pl.BlockSpec((tm, tn), lambda i, j: (i, j)),
        out_shape=jax.ShapeDtypeStruct((m, n), BF16),
        scratch_shapes=[pltpu.VMEM((tm, D_INNER), BF16)],
        compiler_params=_params(("parallel", "arbitrary")),
        name="merge_branches",
    )(oa, y, xs, dskip_exp.reshape(1, D_INNER), zdt, ssm_norm_g.reshape(1, D_INNER), oc,
      w0, w1, w2, gates, gates, gates)


def _rmsnorm_kernel(x_ref, g_ref, o_ref):
    x = x_ref[...]
    o_ref[...] = x * lax.rsqrt(jnp.mean(x * x, axis=-1, keepdims=True) + EPS) * g_ref[...]


def rmsnorm_rows(x, g, *, tm=512):
    m, k = x.shape
    tm = min(tm, m)
    return pl.pallas_call(
        _rmsnorm_kernel,
        grid=(m // tm,),
        in_specs=[pl.BlockSpec((tm, k), lambda i: (i, 0)), pl.BlockSpec((1, k), lambda i: (0, 0))],
        out_specs=pl.BlockSpec((tm, k), lambda i: (i, 0)),
        out_shape=jax.ShapeDtypeStruct((m, k), F32),
        compiler_params=_params(("parallel",)),
        name="final_rmsnorm",
    )(x, g.reshape(1, k))


def _rope_tables(n_lat, n_ctx):
    rows = n_lat // GRID_W
    row = jnp.repeat(jnp.arange(rows), GRID_W).astype(F32)
    col = jnp.tile(jnp.arange(GRID_W), rows).astype(F32)
    n_freq = DA // 4
    inv = 1.0 / (ROPE_BASE ** (jnp.arange(n_freq, dtype=F32) / n_freq))
    ang = jnp.concatenate([row[:, None] * inv, col[:, None] * inv], axis=-1)
    reps = LANE // (DA // 2)
    cos, sin = jnp.tile(jnp.cos(ang), (1, reps)), jnp.tile(jnp.sin(ang), (1, reps))
    first = (jnp.arange(LANE) % DA) < DA // 2
    pad = lambda t, v: jnp.concatenate([t, jnp.full((n_ctx, LANE), v, F32)], axis=0)
    return (pad(cos, 1.0), pad(jnp.where(first, -sin, 0.0), 0.0),
            pad(jnp.where(first, 0.0, sin), 0.0))


def _kv_block(n_kv):
    return max(b for b in range(LANE, KV_BLOCK_MAX + 1, LANE) if n_kv % b == 0)


def _pad_last(t, width):
    return jnp.pad(t, [(0, 0)] * (t.ndim - 1) + [(0, width - t.shape[-1])])


def _layout_w_in(w):
    offs, acc = [], 0
    for sz in IN_SIZES[:-1]:
        acc += sz
        offs.append(acc)
    q, k, v, z, xbc, dt, cq, ckv, kr, gates = jnp.split(w, offs, axis=-1)
    cat = lambda *ts: jnp.concatenate(ts, axis=-1).astype(BF16)
    return dict(qk=cat(q, k), v=v.astype(BF16), xbc=xbc.astype(BF16),
                zdt=cat(z, _pad_last(dt, LANE)), mla=cat(cq, ckv),
                kr=_pad_last(kr, LANE).astype(BF16), gates=gates.astype(BF16))


def _layout_w_uq(w):
    w = w.reshape(Q_LORA, H_C, D_NOPE + D_ROPE)
    return _pad_last(w, 2 * LANE).reshape(Q_LORA, H_C * 2 * LANE).astype(BF16)


def _layout_w_ukv(w):
    w = w.reshape(KV_LORA, H_C, D_NOPE + D_VC)
    return (w[..., :D_NOPE].reshape(KV_LORA, H_C * D_NOPE).astype(BF16),
            w[..., D_NOPE:].reshape(KV_LORA, H_C * D_VC).astype(BF16))


def kernel(x, c, ctx, c_ctx, norm1_g, norm2_g, w_ada, b_ada, w_in, lam_qk, subln_g, w_o_diff,
           conv_w, conv_b, a_log, dt_bias, d_skip, ssm_norm_g, w_o_ssm, q_norm_g, w_uq,
           kv_norm_g, w_ukv, w_o_mla, w_out, w_mlp1, w_mlp2, final_norm_g):
    n_lat, n_ctx = x.shape[1], ctx.shape[1]
    t_all = n_lat + n_ctx
    tables = _rope_tables(n_lat, n_ctx)
    xcat = jnp.concatenate([x[0], ctx[0]], axis=0)
    zero_d = jnp.zeros((D_MODEL,), F32)
    cs_diff = jnp.concatenate([jnp.full((2 * H_A * DA,), (DA ** -0.5) * LOG2E, F32),
                               jnp.ones((2 * H_A * DA,), F32)])
    cs_mla = jnp.full((H_C * 2 * LANE,), MLA_SCALE * LOG2E, F32)
    ones_l = jnp.ones((LANE,), F32)
    both = lambda v: jnp.stack([v, v])

    for l in range(DEPTH):
        last = l == DEPTH - 1
        lam_init = 0.8 - 0.6 * math.exp(-0.3 * l)
        a_neg = -jnp.exp(a_log[l].astype(F32))

        cond = jnp.concatenate([c, c_ctx[None], jnp.zeros((6, D_MODEL), F32)], axis=0)
        cond = cond * jax.nn.sigmoid(cond)
        nmod = N_MOD * D_MODEL
        mod = matmul_epilogue(cond, w_ada[l], jnp.zeros((8, nmod), F32), jnp.ones((nmod,), F32),
                              b_ada[l], tm=8, tn=1024, tk=D_MODEL)
        mx = [mod[0, i * D_MODEL:(i + 1) * D_MODEL] for i in range(N_MOD)]
        mc = [mod[1, i * D_MODEL:(i + 1) * D_MODEL] for i in range(N_MOD)]
        sc1, sh1 = mod[0:2, D_MODEL:2 * D_MODEL], mod[0:2, 0:D_MODEL]

        w_in_l = _layout_w_in(w_in[l])
        w_uq_l = _layout_w_uq(w_uq[l])
        w_uk_l, w_uv_l = _layout_w_ukv(w_ukv[l])
        w_od, w_os, w_om = (w.astype(BF16) for w in (w_o_diff[l], w_o_ssm[l], w_o_mla[l]))
        w_out_l, w1_l, w2_l = (w.astype(BF16) for w in (w_out[l], w_mlp1[l], w_mlp2[l]))
        dskip_exp = jnp.repeat(d_skip[l], P_M)
        z2q, z2kv = jnp.zeros((2, Q_LORA), F32), jnp.zeros((2, KV_LORA), F32)

        h1 = rownorm(xcat, norm1_g[l], sc1, sh1, n_lat=n_lat, tm=640)
        inp = functools.partial(proj, h1, tm=1280)
        qk = inp(w_in_l["qk"], tn=512, out_dtype=BF16, epilogue="rope", colscale=cs_diff,
                 tables=tables)
        vt = inp(w_in_l["v"], tn=512, out_dtype=BF16, epilogue="transpose")
        xbc = inp(w_in_l["xbc"], tn=768, out_dtype=F32)
        zdt = inp(w_in_l["zdt"], tn=1152, out_dtype=F32)
        mla = inp(w_in_l["mla"], tn=768, out_dtype=F32)
        kr = inp(w_in_l["kr"], tn=LANE, out_dtype=BF16, epilogue="rope", colscale=ones_l,
                 tables=tables)
        gates = inp(w_in_l["gates"], tn=1024, out_dtype=F32)

        up = functools.partial(proj, mla, n_lat=n_lat, tm=640)
        mq = up(w_uq_l, q_norm_g[l], z2q, z2q, xcb=0, tn=1024, out_dtype=BF16, epilogue="rope",
                colscale=cs_mla, tables=tables, rope_every=2)
        kn = up(w_uk_l, kv_norm_g[l], z2kv, z2kv, xcb=Q_LORA // KV_LORA, tn=1024, out_dtype=BF16)
        mvt = up(w_uv_l, kv_norm_g[l], z2kv, z2kv, xcb=Q_LORA // KV_LORA, tn=512,
                 out_dtype=BF16, epilogue="transpose")

        xs, bc, a_exp, dt_exp, a_t = ssm_prep(xbc, zdt, conv_w[l], conv_b[l], dt_bias[l], a_neg,
                                              n_lat=n_lat)
        y = ssd_scan(xs, bc, a_exp, dt_exp, a_t, n_lat=n_lat)

        def tail(n_q, q_row0, n_kv, kv_row0, res, m, tq, tk, tm):
            oa = diff_flash_attention(lam_qk[l], subln_g[l], qk, vt, lam_init=lam_init, n_q=n_q,
                                      q_row0=q_row0, n_kv=n_kv, kv_row0=kv_row0, tq=tq, tk=tk)
            oc = mla_flash_attention(mq, kn, kr, mvt, n_q=n_q, q_row0=q_row0, n_kv=n_kv,
                                     kv_row0=kv_row0, tq=tq, tk=tk)
            merged = merge_branches(oa, oc, y, xs, zdt, gates, dskip_exp, ssm_norm_g[l],
                                    w_od, w_os, w_om, row0=q_row0)
            s1 = matmul_epilogue(merged, w_out_l, res, m[2], zero_d, tm=tm, tn=1024, tk=D_MODEL)
            hmid = proj(s1, w1_l, norm2_g[l], both(m[4]), both(m[3]), n_lat=n_q, tm=tm, tn=1024,
                        out_dtype=BF16, epilogue="relu2")
            return matmul_epilogue(hmid, w2_l, s1, m[5], zero_d, tm=tm, tn=1024, tk=2048)

        new_lat = tail(n_lat, 0, t_all, 0, xcat, mx, 512, _kv_block(t_all), 512)
        if not last:
            new_cx = tail(n_ctx, n_lat, n_ctx, n_lat, xcat[n_lat:], mc, n_ctx, n_ctx, n_ctx)
            xcat = jnp.concatenate([new_lat, new_cx], axis=0)

    return rmsnorm_rows(new_lat, final_norm_g)[None]
```
